```python
import jax, jax.numpy as jnp
from jax import lax
import numpy as np

D_MODEL = 1024
BATCH = 4
SEQ = 4096
DEPTH = 4
DEC_BATCH = 128
DEC_SEQ = 4
PAST_LEN = 2048
PAGE_SIZE = 128

N_A_LAYERS = DEPTH // 2
N_B_LAYERS = DEPTH - N_A_LAYERS
D_RNN = D_MODEL
N_LRU_BLOCKS = 4
LRU_BLOCK = D_RNN // N_LRU_BLOCKS
LRU_C = 8.0
CONV_A_WIDTH = 4
D_FF = 3 * D_MODEL
FFN_CONV_WIDTH = 3
HEAD_DIM = 64
N_KV_HEADS = 8
WINDOWS = (128, 512, 2048)
DILATIONS = (1, 4, 16)
N_GROUPS = len(WINDOWS)
MAX_WINDOW = max(WINDOWS)
BLOCK = 128
ROT_DIM = HEAD_DIM // 4
ROPE_THETA = 500000.0
EPS = 1e-6

kernel_name = 'dual_path_rglru_dilated_swa_step'


def rmsnorm(x, g):
    xf = x.astype(jnp.float32)
    y = xf * lax.rsqrt(jnp.mean(xf * xf, axis=-1, keepdims=True) + EPS)
    return (y * g.astype(jnp.float32)).astype(x.dtype)


def rope(x, pos):
    half = ROT_DIM // 2
    inv = ROPE_THETA ** (-jnp.arange(half, dtype=jnp.float32) * (2.0 / ROT_DIM))
    ang = pos.astype(jnp.float32)[:, None] * inv[None, :]
    shape = (1, pos.shape[0]) + (1,) * (x.ndim - 3) + (half,)
    cos = jnp.cos(ang).reshape(shape)
    sin = jnp.sin(ang).reshape(shape)
    xf = x.astype(jnp.float32)
    x1, x2, rest = xf[..., :half], xf[..., half:ROT_DIM], xf[..., ROT_DIM:]
    out = jnp.concatenate([x1 * cos - x2 * sin, x2 * cos + x1 * sin, rest], axis=-1)
    return out.astype(x.dtype)


def causal_dwconv(prev, x, w, b):
    K = w.shape[0]
    T = x.shape[1]
    xc = jnp.concatenate([prev.astype(x.dtype), x], axis=1)
    y = b + sum(xc[:, j:j + T] * w[j] for j in range(K))
    return y, xc[:, T:]


def _lin_combine(e1, e2):
    a1, b1 = e1
    a2, b2 = e2
    return a1 * a2, a2 * b1 + b2


def recurrent_block(h, conv_prev, h0, w_in, conv_w, conv_b, r_w, r_b, i_w, i_b, lam, w_out):
    B, T, _ = h.shape
    gate, xb = jnp.split(h @ w_in, 2, axis=-1)
    xc, conv_state = causal_dwconv(conv_prev, xb, conv_w, conv_b)
    xf = xc.astype(jnp.float32)
    xblk = xf.reshape(B, T, N_LRU_BLOCKS, LRU_BLOCK)
    r = jax.nn.sigmoid(jnp.einsum('btnc,ncd->btnd', xblk, r_w.astype(jnp.float32)) + r_b.astype(jnp.float32)).reshape(B, T, D_RNN)
    i = jax.nn.sigmoid(jnp.einsum('btnc,ncd->btnd', xblk, i_w.astype(jnp.float32)) + i_b.astype(jnp.float32)).reshape(B, T, D_RNN)
    log_a = -LRU_C * r * jax.nn.softplus(-lam.astype(jnp.float32))
    a = jnp.exp(log_a)
    u = jnp.sqrt(-jnp.expm1(2.0 * log_a)) * (i * xf)
    a_cum, u_cum = lax.associative_scan(_lin_combine, (a, u), axis=1)
    hs = a_cum * h0.astype(jnp.float32)[:, None] + u_cum
    y = hs.astype(h.dtype) * jax.nn.gelu(gate)
    return y @ w_out, conv_state, hs[:, -1].astype(h.dtype)


def conv_ffn(h, prev, w_up, conv_w, conv_b, w_down):
    g, v = jnp.split(h @ w_up, 2, axis=-1)
    gc, new_prev = causal_dwconv(prev, g, conv_w, conv_b)
    return (jax.nn.gelu(gc) * v) @ w_down, new_prev


def shared_kv(x, pos, g_kv, w_kv, g_k):
    B, T, _ = x.shape
    kv = (rmsnorm(x, g_kv) @ w_kv).reshape(B, T, 2, N_KV_HEADS, HEAD_DIM)
    k = rope(rmsnorm(kv[:, :, 0], g_k), pos)
    return k, kv[:, :, 1]


def queries(h, pos, w_q, g_q):
    B, T, _ = h.shape
    q = (h @ w_q).reshape(B, T, N_GROUPS, N_KV_HEADS, HEAD_DIM)
    return rope(rmsnorm(q, g_q), pos)


def dilated_band_attn(q, k, v, dilation, span):
    B, T, H, E = q.shape
    L = T // dilation
    nb = -(-L // BLOCK)
    Lp = nb * BLOCK

    def to_res(t):
        t = jnp.moveaxis(t.reshape(B, L, dilation, H, E), 2, 1)
        t = jnp.pad(t, ((0, 0), (0, 0), (0, Lp - L), (0, 0), (0, 0)))
        return t.reshape(B, dilation, nb, BLOCK, H, E)

    def with_prev(t):
        prev = jnp.pad(t, ((0, 0), (0, 0), (1, 0), (0, 0), (0, 0), (0, 0)))[:, :, :-1]
        return jnp.concatenate([prev, t], axis=3)

    qb = to_res(q)
    kk = with_prev(to_res(k))
    vv = with_prev(to_res(v))
    s = jnp.einsum('bdnqhe,bdnkhe->bdnhqk', qb, kk, preferred_element_type=jnp.float32) * (HEAD_DIM ** -0.5)
    qi = jnp.arange(BLOCK)[:, None]
    ki = jnp.arange(2 * BLOCK)[None, :]
    dist = BLOCK + qi - ki
    mq = jnp.arange(nb)[:, None, None] * BLOCK + qi
    valid = (dist >= 0) & (dist <= span) & (mq - dist >= 0)
    s = jnp.where(valid[:, None], s, -jnp.inf)
    m = jnp.max(s, axis=-1, keepdims=True)
    p = jnp.exp(s - m)
    den = jnp.sum(p, axis=-1)
    o = jnp.einsum('bdnhqk,bdnkhe->bdnqhe', p, vv, preferred_element_type=jnp.float32)
    o = o / jnp.swapaxes(den, -1, -2)[..., None]
    lse = jnp.swapaxes(m[..., 0] + jnp.log(den), -1, -2)

    def back(t):
        t = t.reshape((B, dilation, Lp) + t.shape[4:])[:, :, :L]
        return jnp.moveaxis(t, 1, 2).reshape((B, T) + t.shape[3:])

    return back(o), back(lse)


def dilated_gather_attn(q, k_all, v_all, n_past, dilation, span):
    S = q.shape[1]
    idx = n_past + jnp.arange(S)[:, None] - dilation * jnp.arange(span + 1)[None, :]
    valid = idx >= 0
    idxc = jnp.maximum(idx, 0)
    kg = k_all[:, idxc]
    vg = v_all[:, idxc]
    s = jnp.einsum('bshe,bsjhe->bshj', q, kg, preferred_element_type=jnp.float32) * (HEAD_DIM ** -0.5)
    s = jnp.where(valid[:, None, :], s, -jnp.inf)
    m = jnp.max(s, axis=-1, keepdims=True)
    p = jnp.exp(s - m)
    den = jnp.sum(p, axis=-1)
    o = jnp.einsum('bshj,bsjhe->bshe', p, vg, preferred_element_type=jnp.float32) / den[..., None]
    return o, m[..., 0] + jnp.log(den)


def combine_groups(outs, lses, dtype):
    w = jax.nn.softmax(jnp.stack(lses), axis=0)
    o = jnp.sum(w[..., None] * jnp.stack(outs), axis=0)
    B, T, H, E = o.shape
    return o.reshape(B, T, H * E).astype(dtype)


def mixture_prompt(q, k, v):
    outs, lses = [], []
    for g in range(N_GROUPS):
        o, l = dilated_band_attn(q[:, :, g], k, v, DILATIONS[g], WINDOWS[g] // DILATIONS[g])
        outs.append(o)
        lses.append(l)
    return combine_groups(outs, lses, q.dtype)


def mixture_sample(q, k_all, v_all, n_past):
    outs, lses = [], []
    for g in range(N_GROUPS):
        o, l = dilated_gather_attn(q[:, :, g], k_all, v_all, n_past, DILATIONS[g], WINDOWS[g] // DILATIONS[g])
        outs.append(o)
        lses.append(l)
    return combine_groups(outs, lses, q.dtype)


def trunk(x, pos, conv_a_prev, h0, ffn_prev, extend_kv, attend, p):
    hs_new, conv_new, ffn_new = [], [], []
    k = v = k_att = v_att = None
    for layer in range(DEPTH):
        if layer < N_A_LAYERS:
            out, cs, hl = recurrent_block(
                rmsnorm(x, p['norm_mix_a'][layer]), conv_a_prev[layer], h0[layer],
                p['w_in_a'][layer], p['conv_a_w'][layer], p['conv_a_b'][layer],
                p['gate_r_w'][layer], p['gate_r_b'][layer], p['gate_i_w'][layer], p['gate_i_b'][layer],
                p['lru_lambda'][layer], p['w_out_a'][layer])
            conv_new.append(cs)
            hs_new.append(hl)
        else:
            j = layer - N_A_LAYERS
            q = queries(rmsnorm(x, p['norm_mix_b'][j]), pos, p['w_q'][j], p['q_norm'][j])
            out = attend(q, k_att, v_att) @ p['w_o'][j]
        x = x + out
        f, fs = conv_ffn(rmsnorm(x, p['norm_ffn'][layer]), ffn_prev[layer], p['w_ffn_up'][layer],
                         p['ffn_conv_w'][layer], p['ffn_conv_b'][layer], p['w_ffn_down'][layer])
        ffn_new.append(fs)
        x = x + f
        if layer == N_A_LAYERS - 1:
            k, v = shared_kv(x, pos, p['norm_kv'], p['w_kv'], p['k_norm'])
            k_att, v_att = extend_kv(k, v)
    return x, k, v, jnp.stack(hs_new), jnp.stack(conv_new), jnp.stack(ffn_new)


def setup_inputs(seed: int = 0) -> dict:
    key = jax.random.key(seed)
    ks = iter(jax.random.split(key, 32))

    def nrm(shape, scale):
        return scale * jax.random.normal(next(ks), shape, jnp.float32)

    w_buf = min(MAX_WINDOW, PAST_LEN)
    a0 = None
    d = {}
    d['x_prompt'] = nrm((BATCH, SEQ, D_MODEL), 1.0)
    d['x_sample'] = nrm((DEC_BATCH, DEC_SEQ, D_MODEL), 1.0)
    d['cache_k'] = nrm((DEC_BATCH, w_buf, N_KV_HEADS, HEAD_DIM), 1.0)
    d['cache_v'] = nrm((DEC_BATCH, w_buf, N_KV_HEADS, HEAD_DIM), 1.0)
    d['state_rglru_h'] = nrm((N_A_LAYERS, DEC_BATCH, D_RNN), 0.5)
    d['state_rglru_conv'] = nrm((N_A_LAYERS, DEC_BATCH, CONV_A_WIDTH - 1, D_RNN), 1.0)
    d['state_ffn_conv'] = nrm((DEPTH, DEC_BATCH, FFN_CONV_WIDTH - 1, D_FF), 1.0)
    d['norm_mix_a'] = 1.0 + nrm((N_A_LAYERS, D_MODEL), 0.1)
    d['w_in_a'] = nrm((N_A_LAYERS, D_MODEL, 2 * D_RNN), D_MODEL ** -0.5)
    d['conv_a_w'] = nrm((N_A_LAYERS, CONV_A_WIDTH, D_RNN), CONV_A_WIDTH ** -0.5)
    d['conv_a_b'] = nrm((N_A_LAYERS, D_RNN), 0.01)
    d['gate_r_w'] = nrm((N_A_LAYERS, N_LRU_BLOCKS, LRU_BLOCK, LRU_BLOCK), LRU_BLOCK ** -0.5)
    d['gate_r_b'] = nrm((N_A_LAYERS, N_LRU_BLOCKS, LRU_BLOCK), 0.01)
    d['gate_i_w'] = nrm((N_A_LAYERS, N_LRU_BLOCKS, LRU_BLOCK, LRU_BLOCK), LRU_BLOCK ** -0.5)
    d['gate_i_b'] = nrm((N_A_LAYERS, N_LRU_BLOCKS, LRU_BLOCK), 0.01)
    a0 = jax.random.uniform(next(ks), (N_A_LAYERS, D_RNN), jnp.float32, minval=0.9, maxval=0.999)
    d['lru_lambda'] = jnp.log(a0) - jnp.log1p(-a0)
    d['w_out_a'] = nrm((N_A_LAYERS, D_RNN, D_MODEL), D_RNN ** -0.5)
    d['norm_kv'] = 1.0 + nrm((D_MODEL,), 0.1)
    d['w_kv'] = nrm((D_MODEL, 2 * N_KV_HEADS * HEAD_DIM), D_MODEL ** -0.5)
    d['k_norm'] = 1.0 + nrm((HEAD_DIM,), 0.1)
    d['norm_mix_b'] = 1.0 + nrm((N_B_LAYERS, D_MODEL), 0.1)
    d['w_q'] = nrm((N_B_LAYERS, D_MODEL, N_GROUPS * N_KV_HEADS * HEAD_DIM), D_MODEL ** -0.5)
    d['q_norm'] = 1.0 + nrm((N_B_LAYERS, HEAD_DIM), 0.1)
    d['w_o'] = nrm((N_B_LAYERS, N_KV_HEADS * HEAD_DIM, D_MODEL), (N_KV_HEADS * HEAD_DIM) ** -0.5)
    d['norm_ffn'] = 1.0 + nrm((DEPTH, D_MODEL), 0.1)
    d['w_ffn_up'] = nrm((DEPTH, D_MODEL, 2 * D_FF), D_MODEL ** -0.5)
    d['ffn_conv_w'] = nrm((DEPTH, FFN_CONV_WIDTH, D_FF), FFN_CONV_WIDTH ** -0.5)
    d['ffn_conv_b'] = nrm((DEPTH, D_FF), 0.01)
    d['w_ffn_down'] = nrm((DEPTH, D_FF, D_MODEL), D_FF ** -0.5)
    return d


def reference(x_prompt, x_sample, cache_k, cache_v, state_rglru_h, state_rglru_conv, state_ffn_conv,
              norm_mix_a, w_in_a, conv_a_w, conv_a_b, gate_r_w, gate_r_b, gate_i_w, gate_i_b,
              lru_lambda, w_out_a, norm_kv, w_kv, k_norm, norm_mix_b, w_q, q_norm, w_o,
              norm_ffn, w_ffn_up, ffn_conv_w, ffn_conv_b, w_ffn_down):
    p = dict(norm_mix_a=norm_mix_a, w_in_a=w_in_a, conv_a_w=conv_a_w, conv_a_b=conv_a_b,
             gate_r_w=gate_r_w, gate_r_b=gate_r_b, gate_i_w=gate_i_w, gate_i_b=gate_i_b,
             lru_lambda=lru_lambda, w_out_a=w_out_a, norm_kv=norm_kv, w_kv=w_kv, k_norm=k_norm,
             norm_mix_b=norm_mix_b, w_q=w_q, q_norm=q_norm, w_o=w_o, norm_ffn=norm_ffn,
             w_ffn_up=w_ffn_up, ffn_conv_w=ffn_conv_w, ffn_conv_b=ffn_conv_b, w_ffn_down=w_ffn_down)

    bp, tp, _ = x_prompt.shape
    dt = x_prompt.dtype
    pos_p = jnp.arange(tp)
    y_prompt, k_p, v_p, p_h, p_conv, p_ffn = trunk(
        x_prompt, pos_p,
        jnp.zeros((N_A_LAYERS, bp, CONV_A_WIDTH - 1, D_RNN), dt),
        jnp.zeros((N_A_LAYERS, bp, D_RNN), jnp.float32),
        jnp.zeros((DEPTH, bp, FFN_CONV_WIDTH - 1, D_FF), dt),
        lambda k, v: (k, v),
        mixture_prompt, p)
    keep = min(MAX_WINDOW, tp)
    p_cache_k = k_p[:, tp - keep:]
    p_cache_v = v_p[:, tp - keep:]

    n_past = cache_k.shape[1]
    pos_s = PAST_LEN + jnp.arange(x_sample.shape[1])
    y_sample, s_cache_k, s_cache_v, s_h, s_conv, s_ffn = trunk(
        x_sample, pos_s, state_rglru_conv, state_rglru_h, state_ffn_conv,
        lambda k, v: (jnp.concatenate([cache_k.astype(k.dtype), k], axis=1),
                      jnp.concatenate([cache_v.astype(v.dtype), v], axis=1)),
        lambda q, ka, va: mixture_sample(q, ka, va, n_past), p)

    return (y_prompt, y_sample, p_h, p_conv, p_ffn, p_cache_k, p_cache_v,
            s_h, s_conv, s_ffn, s_cache_k, s_cache_v)
```

```python
import functools

import numpy as np
import jax
import jax.numpy as jnp
from jax import lax
from jax.experimental import pallas as pl
from jax.experimental.pallas import tpu as pltpu

F32 = jnp.float32
BF16 = jnp.bfloat16

N_LRU_BLOCKS = 4
LRU_C = 8.0
CONV_A_WIDTH = 4
FFN_CONV_WIDTH = 3
HEAD_DIM = 64
N_KV_HEADS = 8
KV_DIM = N_KV_HEADS * HEAD_DIM
WINDOWS = (128, 512, 2048)
DILATIONS = (1, 4, 16)
N_GROUPS = len(WINDOWS)
SPAN = 128
ROT_DIM = HEAD_DIM // 4
ROPE_THETA = 500000.0
PAST_LEN = 2048
EPS = 1e-6
SCALE = HEAD_DIM ** -0.5

LANES = 128
SUBLANES = 8
ROW_TILE = 512
FF_CHUNK = 1024
VMEM_LIMIT = 56 * 1024 * 1024

_NT = (((1,), (1,)), ((), ()))


def _rmsnorm(x, g):
    ms = jnp.mean(x * x, axis=-1, keepdims=True)
    return (x * lax.rsqrt(ms + EPS)) * g


def _dot(a, b):
    return jnp.dot(a, b, preferred_element_type=F32)


def _const_spec(shape, single_buffer=False):
    nd = len(shape)
    kw = {"pipeline_mode": pl.Buffered(1)} if single_buffer else {}
    return pl.BlockSpec(shape, lambda *_: (0,) * nd, **kw)


def _params(*sem):
    return pltpu.CompilerParams(dimension_semantics=sem, vmem_limit_bytes=VMEM_LIMIT)


def _conv_pad(width, tstride):
    hist = (width - 1) * tstride
    return hist, max(SUBLANES, hist)


def _causal_conv(cat_ref, cols, w_ref, b_ref, width, tstride, rows):
    hist, pad = _conv_pad(width, tstride)
    acc = None
    for j in range(width):
        start = pad - hist + j * tstride
        term = cat_ref[start:start + rows, cols] * w_ref[j:j + 1, cols]
        acc = term if acc is None else acc + term
    return b_ref[:, cols] + acc


def _mixer_a_kernel(x_ref, cprev_ref, h0_ref, g_ref, win_ref, cw_ref, cb_ref, gw_ref, gb_ref,
                    lam_ref, wout_ref, xo_ref, cst_ref, hl_ref, cat_s, hc_s, a_s, u_s,
                    *, tstride, rows):
    d = x_ref.shape[-1]
    blk = d // N_LRU_BLOCKS
    hist, pad = _conv_pad(CONV_A_WIDTH, tstride)

    @pl.when(pl.program_id(1) == 0)
    def _():
        cat_s[pad - hist:pad, :] = cprev_ref[...]
        hc_s[...] = h0_ref[...]

    x = x_ref[...]
    hn = _rmsnorm(x, g_ref[...]).astype(BF16)
    y = _dot(hn, win_ref[...])
    gate = y[:, :d]
    cat_s[pad:pad + rows, :] = y[:, d:]
    xc = _causal_conv(cat_s, slice(None), cw_ref, cb_ref, CONV_A_WIDTH, tstride, rows)
    new_hist = cat_s[pad + rows - hist:pad + rows, :]
    cat_s[pad - hist:pad, :] = new_hist
    cst_ref[...] = new_hist

    lam = lam_ref[...]
    nsp = -lam
    softplus = jnp.maximum(nsp, 0.0) + jnp.log1p(jnp.exp(-jnp.abs(nsp)))
    xcb = xc.astype(BF16)
    for n in range(N_LRU_BLOCKS):
        cols = slice(n * blk, (n + 1) * blk)
        ri = _dot(xcb[:, cols], gw_ref[n]) + gb_ref[n]
        r = jax.nn.sigmoid(ri[:, :blk])
        i = jax.nn.sigmoid(ri[:, blk:])
        log_a = (-LRU_C) * r * softplus[:, cols]
        a = jnp.exp(log_a)
        a_s[:, cols] = a
        u_s[:, cols] = jnp.sqrt(1.0 - a * a) * (i * xc[:, cols])

    if tstride == 1:
        groups = rows // SUBLANES
        a3 = a_s[...].reshape(groups, SUBLANES, d)
        u3 = u_s[...].reshape(groups, SUBLANES, d)
        sub = lax.broadcasted_iota(jnp.int32, a3.shape, 1)
        for k in (1, 2, 4):
            keep = sub >= k
            u3 = jnp.where(keep, a3 * pltpu.roll(u3, k, 1) + u3, u3)
            a3 = jnp.where(keep, a3 * pltpu.roll(a3, k, 1), a3)
        a_s[...] = a3.reshape(rows, d)
        u_s[...] = u3.reshape(rows, d)

        def carry(gi, hb):
            r0 = pl.multiple_of(gi * SUBLANES, SUBLANES)
            h = a_s[pl.ds(r0, SUBLANES), :] * hb + u_s[pl.ds(r0, SUBLANES), :]
            u_s[pl.ds(r0, SUBLANES), :] = h
            return jnp.broadcast_to(h[SUBLANES - 1:SUBLANES, :], (SUBLANES, d))

        hb = lax.fori_loop(0, groups, carry, jnp.broadcast_to(hc_s[...], (SUBLANES, d)))
        h_last = hb[0:1, :]
    else:
        h_last = hc_s[...]
        for t in range(rows // tstride):
            sl = slice(t * tstride, (t + 1) * tstride)
            h_last = a_s[sl, :] * h_last + u_s[sl, :]
            u_s[sl, :] = h_last
    hc_s[...] = h_last
    hl_ref[...] = h_last

    yv = (u_s[...] * jax.nn.gelu(gate)).astype(BF16)
    xo_ref[...] = x + _dot(yv, wout_ref[...])


def _mixer_a(x, cprev, h0, g, win, cw, cb, gw, gb, lam, wout, *, tstride):
    nb, t, d = x.shape
    rows = min(ROW_TILE, t)
    hist, pad = _conv_pad(CONV_A_WIDTH, tstride)
    row_spec = pl.BlockSpec((None, rows, d), lambda b, i: (b, i, 0))
    per_b = lambda n: pl.BlockSpec((None, n, d), lambda b, i: (b, 0, 0))
    return pl.pallas_call(
        functools.partial(_mixer_a_kernel, tstride=tstride, rows=rows),
        grid=(nb, t // rows),
        in_specs=[row_spec, per_b(hist), per_b(tstride), _const_spec(g.shape),
                  _const_spec(win.shape, True), _const_spec(cw.shape), _const_spec(cb.shape),
                  _const_spec(gw.shape, True), _const_spec(gb.shape), _const_spec(lam.shape),
                  _const_spec(wout.shape, True)],
        out_specs=[row_spec, per_b(hist), per_b(tstride)],
        out_shape=[jax.ShapeDtypeStruct(x.shape, F32),
                   jax.ShapeDtypeStruct((nb, hist, d), F32),
                   jax.ShapeDtypeStruct((nb, tstride, d), F32)],
        scratch_shapes=[pltpu.VMEM((pad + rows, d), F32), pltpu.VMEM((tstride, d), F32),
                        pltpu.VMEM((rows, d), F32), pltpu.VMEM((rows, d), F32)],
        compiler_params=_params("parallel", "arbitrary"),
        name="mixer_a",
    )(x, cprev, h0, g, win, cw, cb, gw, gb, lam, wout)


def _ffn_kernel(*refs, tstride, rows, n_att):
    x_ref = refs[0]
    att = refs[1:1 + 2 * n_att] if n_att > 1 else refs[1:1 + n_att]
    rest = refs[1 + len(att):]
    if n_att:
        wo_ref, rest = rest[0], rest[1:]
    prev_ref, g_ref, wup_ref, cw_ref, cb_ref, wdn_ref, xo_ref, st_ref, cat_s = rest
    dff = wdn_ref.shape[0]
    hist, pad = _conv_pad(FFN_CONV_WIDTH, tstride)

    @pl.when(pl.program_id(1) == 0)
    def _():
        cat_s[pad - hist:pad, :] = prev_ref[...]

    x = x_ref[...]
    if n_att == 1:
        x = x + _dot(att[0][...].astype(BF16), wo_ref[...])
    elif n_att > 1:
        outs = [r[...] for r in att[:n_att]]
        lses = [r[...] for r in att[n_att:]]
        top = functools.reduce(jnp.maximum, lses)
        es = [jnp.exp(l - top) for l in lses]
        num = functools.reduce(jnp.add, [e * o for e, o in zip(es, outs)])
        merged = num / functools.reduce(jnp.add, es)
        x = x + _dot(merged.astype(BF16), wo_ref[...])

    hn = _rmsnorm(x, g_ref[...]).astype(BF16)
    acc = None
    for c in range(dff // FF_CHUNK):
        cols = slice(c * FF_CHUNK, (c + 1) * FF_CHUNK)
        vcols = slice(dff + c * FF_CHUNK, dff + (c + 1) * FF_CHUNK)
        cat_s[pad:pad + rows, cols] = _dot(hn, wup_ref[:, cols])
        gc = _causal_conv(cat_s, cols, cw_ref, cb_ref, FFN_CONV_WIDTH, tstride, rows)
        act = (jax.nn.gelu(gc) * _dot(hn, wup_ref[:, vcols])).astype(BF16)
        part = _dot(act, wdn_ref[cols, :])
        acc = part if acc is None else acc + part
    new_hist = cat_s[pad + rows - hist:pad + rows, :]
    cat_s[pad - hist:pad, :] = new_hist
    st_ref[...] = new_hist
    xo_ref[...] = x + acc


def _conv_ffn(x, att, wo, prev, g, wup, cw, cb, wdn, *, tstride):
    nb, t, d = x.shape
    dff = wdn.shape[0]
    rows = min(ROW_TILE, t)
    hist, pad = _conv_pad(FFN_CONV_WIDTH, tstride)
    row_spec = pl.BlockSpec((None, rows, d), lambda b, i: (b, i, 0))
    att_spec = pl.BlockSpec((None, rows, KV_DIM), lambda b, i: (b, i, 0))
    st_spec = pl.BlockSpec((None, hist, dff), lambda b, i: (b, 0, 0))
    n_att = {0: 0, 1: 1, 2 * N_GROUPS: N_GROUPS}[len(att)]
    wo_args, wo_specs = ((wo,), [_const_spec(wo.shape)]) if n_att else ((), [])
    return pl.pallas_call(
        functools.partial(_ffn_kernel, tstride=tstride, rows=rows, n_att=n_att),
        grid=(nb, t // rows),
        in_specs=[row_spec] + [att_spec] * len(att) + wo_specs
        + [st_spec, _const_spec(g.shape), _const_spec(wup.shape, True), _const_spec(cw.shape),
           _const_spec(cb.shape), _const_spec(wdn.shape, True)],
        out_specs=[row_spec, st_spec],
        out_shape=[jax.ShapeDtypeStruct(x.shape, F32), jax.ShapeDtypeStruct((nb, hist, dff), F32)],
        scratch_shapes=[pltpu.VMEM((pad + rows, dff), F32)],
        compiler_params=_params("parallel", "arbitrary"),
        name="conv_ffn",
    )(x, *att, *wo_args, prev, g, wup, cw, cb, wdn)


def _proj_kernel(x_ref, g_ref, w_ref, hsum_ref, gh_ref, cos_ref, sin_ref, *out_refs, n_rope):
    hn = _rmsnorm(x_ref[...], g_ref[...]).astype(BF16)
    y = _dot(hn, w_ref[...])
    rows = y.shape[0]
    lane = lax.broadcasted_iota(jnp.int32, (rows, LANES), 1)
    first_half = (lane & (HEAD_DIM - 1)) < (ROT_DIM // 2)
    hsum = hsum_ref[...]
    cos = cos_ref[...]
    sin = sin_ref[...]
    for i in range(n_rope // LANES):
        cols = slice(i * LANES, (i + 1) * LANES)
        yt = y[:, cols]
        sq = yt * yt
        hi = sq.astype(BF16)
        lo = (sq - hi.astype(F32)).astype(BF16)
        ms = _dot(hi, hsum) + _dot(lo, hsum)
        yn = (yt * lax.rsqrt(ms + EPS)) * gh_ref[...]
        partner = jnp.where(first_half, pltpu.roll(yn, LANES - ROT_DIM // 2, 1),
                            pltpu.roll(yn, ROT_DIM // 2, 1))
        out_refs[0][:, cols] = yn * cos + partner * sin
    if len(out_refs) > 1:
        out_refs[1][...] = y[:, n_rope:]


def _project(x, g, w, hsum, gh, cos, sin, *, n_rope):
    nb, t, d = x.shape
    n = w.shape[1]
    rows = min(ROW_TILE, t)
    row_in = pl.BlockSpec((None, rows, d), lambda b, i: (b, i, 0))
    tab = pl.BlockSpec((rows, LANES), lambda b, i: (i, 0))
    widths = [n_rope] + ([n - n_rope] if n > n_rope else [])
    return pl.pallas_call(
        functools.partial(_proj_kernel, n_rope=n_rope),
        grid=(nb, t // rows),
        in_specs=[row_in, _const_spec(g.shape), _const_spec(w.shape), _const_spec(hsum.shape),
                  _const_spec(gh.shape), tab, tab],
        out_specs=[pl.BlockSpec((None, rows, wd), lambda b, i: (b, i, 0)) for wd in widths],
        out_shape=[jax.ShapeDtypeStruct((nb, t, wd), F32) for wd in widths],
        compiler_params=_params("parallel", "parallel"),
        name="norm_proj_rope",
    )(x, g, w, hsum, gh, cos, sin)


def _attn_prompt_kernel(q_ref, kc_ref, kp_ref, vc_ref, vp_ref, o_ref, l_ref):
    blk = q_ref.shape[0]
    q = (q_ref[...] * SCALE).astype(BF16)
    k = jnp.concatenate([kp_ref[...], kc_ref[...]], axis=0).astype(BF16)
    v = jnp.concatenate([vp_ref[...], vc_ref[...]], axis=0).astype(BF16)
    qi = lax.broadcasted_iota(jnp.int32, (2 * blk, 2 * blk), 0) & (blk - 1)
    kj = lax.broadcasted_iota(jnp.int32, (2 * blk, 2 * blk), 1)
    dist = blk + qi - kj
    first_key = jnp.where(pl.program_id(2) > 0, 0, blk)
    valid = (dist >= 0) & (dist <= SPAN) & (kj >= first_key)
    low = lax.broadcasted_iota(jnp.int32, (blk, LANES), 1) < HEAD_DIM
    for p in range(KV_DIM // LANES):
        cols = slice(p * LANES, (p + 1) * LANES)
        qp = q[:, cols]
        zero = jnp.zeros_like(qp)
        qs = jnp.concatenate([jnp.where(low, qp, zero), jnp.where(low, zero, qp)], axis=0)
        s = lax.dot_general(qs, k[:, cols], _NT, preferred_element_type=F32)
        s = jnp.where(valid, s, -jnp.inf)
        m = jnp.max(s, axis=-1, keepdims=True)
        pe = jnp.exp(s - m)
        den = jnp.sum(pe, axis=-1, keepdims=True)
        o = _dot(pe.astype(BF16), v[:, cols]) / den
        lse = jnp.broadcast_to(m + jnp.log(den), o.shape)
        o_ref[:, cols] = jnp.where(low, o[:blk], o[blk:])
        l_ref[:, cols] = jnp.where(low, lse[:blk], lse[blk:])


def _attn_prompt(q, k, v, group):
    nb, t, _ = k.shape
    dil = DILATIONS[group]
    sub = t // dil
    blk = SPAN
    qv = q.reshape(nb, sub, dil * N_GROUPS * KV_DIM)
    kv = k.reshape(nb, sub, dil * KV_DIM)
    vv = v.reshape(nb, sub, dil * KV_DIM)
    cur = pl.BlockSpec((None, blk, KV_DIM), lambda b, r, i: (b, i, r))
    prev = pl.BlockSpec((None, blk, KV_DIM), lambda b, r, i: (b, jnp.maximum(i - 1, 0), r))
    qspec = pl.BlockSpec((None, blk, KV_DIM), lambda b, r, i: (b, i, r * N_GROUPS + group))
    o, l = pl.pallas_call(
        _attn_prompt_kernel,
        grid=(nb, dil, sub // blk),
        in_specs=[qspec, cur, prev, cur, prev],
        out_specs=[cur, cur],
        out_shape=[jax.ShapeDtypeStruct(kv.shape, F32)] * 2,
        compiler_params=_params("parallel", "parallel", "parallel"),
        name=f"attn_prompt_g{group}",
    )(qv, kv, kv, vv, vv)
    return o.reshape(nb, t, KV_DIM), l.reshape(nb, t, KV_DIM)


def _attn_sample_kernel(q_ref, kn_ref, vn_ref, k0_ref, k1_ref, k2_ref, v0_ref, v1_ref, v2_ref,
                        o_ref, kn_s, vn_s):
    steps = q_ref.shape[0]
    nrow = steps * N_KV_HEADS
    q = q_ref[...] * SCALE
    kn_s[...] = jnp.zeros_like(kn_s)
    vn_s[...] = jnp.zeros_like(vn_s)
    kn_s[0:steps, :] = kn_ref[...]
    vn_s[0:steps, :] = vn_ref[...]
    kn = kn_s[...].astype(BF16)
    vn = vn_s[...].astype(BF16)

    row = lax.broadcasted_iota(jnp.int32, (nrow, KV_DIM), 0)
    lane = lax.broadcasted_iota(jnp.int32, (nrow, KV_DIM), 1)
    head_lanes = (lane >> 6) == (row & (N_KV_HEADS - 1))
    srow = row >> 3

    def q_rows(g):
        qg = q[:, g * KV_DIM:(g + 1) * KV_DIM]
        rep = jnp.concatenate(
            [jnp.broadcast_to(qg[s:s + 1, :], (N_KV_HEADS, KV_DIM)) for s in range(steps)], axis=0)
        return jnp.where(head_lanes, rep, 0.0)

    def by_step(a):
        return jnp.concatenate([jnp.where(srow == r, a, 0.0) for r in range(steps)], axis=1)

    def fold_step(a):
        parts = [jnp.where(srow == r, a[:, r * KV_DIM:(r + 1) * KV_DIM], 0.0) for r in range(steps)]
        return functools.reduce(jnp.add, parts)

    qg = [q_rows(g) for g in range(N_GROUPS)]
    col = lax.broadcasted_iota(jnp.int32, (nrow, SPAN), 1)
    scol = lax.broadcasted_iota(jnp.int32, (nrow, SPAN), 0) >> 3
    neg = -jnp.inf

    s_win = [
        jnp.where(col >= scol, lax.dot_general(qg[0].astype(BF16), k0_ref[...].astype(BF16), _NT,
                                               preferred_element_type=F32), neg),
        lax.dot_general(by_step(qg[1]).astype(BF16), k1_ref[...].astype(BF16), _NT,
                        preferred_element_type=F32),
        lax.dot_general(by_step(qg[2]).astype(BF16), k2_ref[...].astype(BF16), _NT,
                        preferred_element_type=F32),
    ]
    new_ok = [(col <= scol) if dil == 1 else (col == scol) for dil in DILATIONS]
    s_new = [jnp.where(new_ok[g], lax.dot_general(qg[g].astype(BF16), kn, _NT,
                                                  preferred_element_type=F32), neg)
             for g in range(N_GROUPS)]

    top = functools.reduce(jnp.maximum, [jnp.max(s, axis=-1, keepdims=True) for s in s_win + s_new])
    p_win = [jnp.exp(s - top) for s in s_win]
    p_new = functools.reduce(jnp.add, [jnp.exp(s - top) for s in s_new])
    den = functools.reduce(jnp.add, [jnp.sum(p, axis=-1, keepdims=True) for p in p_win + [p_new]])

    acc = _dot(p_win[0].astype(BF16), v0_ref[...].astype(BF16))
    acc = acc + fold_step(_dot(p_win[1].astype(BF16), v1_ref[...].astype(BF16)))
    acc = acc + fold_step(_dot(p_win[2].astype(BF16), v2_ref[...].astype(BF16)))
    acc = acc + _dot(p_new.astype(BF16), vn)
    acc = jnp.where(head_lanes, acc / den, 0.0)
    o_ref[...] = jnp.sum(acc.reshape(steps, N_KV_HEADS, KV_DIM), axis=1)


def _attn_sample(q, kn, vn, cache_k, cache_v):
    nb, steps, _ = kn.shape
    past = cache_k.shape[1]
    assert past == WINDOWS[-1] and steps <= DILATIONS[1], (past, steps)
    ck = cache_k.reshape(nb, past, KV_DIM)
    cv = cache_v.reshape(nb, past, KV_DIM)
    views, specs = [], []
    for c in (ck, cv):
        views += [c, c.reshape(nb, past // DILATIONS[1], DILATIONS[1] * KV_DIM),
                  c.reshape(nb, past // DILATIONS[2], DILATIONS[2] * KV_DIM)]
        specs += [pl.BlockSpec((None, SPAN, KV_DIM), lambda b: (b, past // SPAN - 1, 0)),
                  pl.BlockSpec((None, SPAN, steps * KV_DIM),
                               lambda b: (b, past // DILATIONS[1] // SPAN - 1, 0)),
                  pl.BlockSpec((None, SPAN, steps * KV_DIM), lambda b: (b, 0, 0))]
    new_spec = pl.BlockSpec((None, steps, KV_DIM), lambda b: (b, 0, 0))
    return pl.pallas_call(
        _attn_sample_kernel,
        grid=(nb,),
        in_specs=[pl.BlockSpec((None, steps, N_GROUPS * KV_DIM), lambda b: (b, 0, 0)),
                  new_spec, new_spec] + specs,
        out_specs=new_spec,
        out_shape=jax.ShapeDtypeStruct((nb, steps, KV_DIM), F32),
        scratch_shapes=[pltpu.VMEM((SPAN, KV_DIM), F32), pltpu.VMEM((SPAN, KV_DIM), F32)],
        compiler_params=_params("parallel"),
        name="attn_sample",
    )(q, kn, vn, *views)


def _rope_tables(pos):
    half = ROT_DIM // 2
    inv = ROPE_THETA ** (-jnp.arange(half, dtype=F32) * (2.0 / ROT_DIM))
    dlane = np.arange(LANES) % HEAD_DIM
    inv_lane = jnp.where(dlane < ROT_DIM, inv[dlane % half], 0.0)
    ang = pos.astype(F32)[:, None] * inv_lane[None, :]
    sign = np.where(dlane < half, -1.0, 1.0).astype(np.float32)
    return jnp.cos(ang), jnp.sin(ang) * sign


def _trunk(x, pos, conv_a_prev, h0, ffn_prev, w, attend, *, tstride):
    n_a = w["w_in"].shape[0]
    depth = w["w_up"].shape[0]
    cos, sin = _rope_tables(pos)
    hs_new, conv_new, ffn_new = [], [], []
    k = v = None
    for layer in range(depth):
        att, wo = (), None
        if layer < n_a:
            x, cs, hl = _mixer_a(x, conv_a_prev[layer], h0[layer], w["norm_a"][layer], w["w_in"][layer],
                                 w["conv_a_w"][layer], w["conv_a_b"][layer], w["gate_w"][layer],
                                 w["gate_b"][layer], w["lam"][layer], w["w_out"][layer], tstride=tstride)
            conv_new.append(cs)
            hs_new.append(hl)
        else:
            j = layer - n_a
            (q,) = _project(x, w["norm_b"][j], w["w_q"][j], w["hsum"], w["q_norm"][j], cos, sin,
                            n_rope=w["w_q"].shape[-1])
            att, wo = attend(q, k, v), w["w_o"][j]
        x, fs = _conv_ffn(x, att, wo, ffn_prev[layer], w["norm_ffn"][layer], w["w_up"][layer],
                          w["ffn_conv_w"][layer], w["ffn_conv_b"][layer], w["w_down"][layer],
                          tstride=tstride)
        ffn_new.append(fs)
        if layer == n_a - 1:
            k, v = _project(x, w["norm_kv"], w["w_kv"], w["hsum"], w["k_norm"], cos, sin, n_rope=KV_DIM)
    return x, k, v, jnp.stack(hs_new), jnp.stack(conv_new), jnp.stack(ffn_new)


def kernel(x_prompt, x_sample, cache_k, cache_v, state_rglru_h, state_rglru_conv, state_ffn_conv, norm_mix_a, w_in_a, conv_a_w, conv_a_b, gate_r_w, gate_r_b, gate_i_w, gate_i_b, lru_lambda, w_out_a, norm_kv, w_kv, k_norm, norm_mix_b, w_q, q_norm, w_o, norm_ffn, w_ffn_up, ffn_conv_w, ffn_conv_b, w_ffn_down):
    n_a, d = norm_mix_a.shape
    depth, _, dff2 = w_ffn_up.shape
    dff = dff2 // 2
    bp, tp, _ = x_prompt.shape
    bs, ts, _ = x_sample.shape

    head_of = np.arange(LANES) // HEAD_DIM
    w = dict(
        norm_a=norm_mix_a[:, None, :], w_in=w_in_a.astype(BF16),
        conv_a_w=conv_a_w, conv_a_b=conv_a_b[:, None, :],
        gate_w=jnp.concatenate([gate_r_w, gate_i_w], axis=-1).astype(BF16),
        gate_b=jnp.concatenate([gate_r_b, gate_i_b], axis=-1)[:, :, None, :],
        lam=lru_lambda[:, None, :], w_out=w_out_a.astype(BF16),
        norm_kv=norm_kv[None, :], w_kv=w_kv.astype(BF16),
        k_norm=jnp.tile(k_norm, LANES // HEAD_DIM)[None, :],
        norm_b=norm_mix_b[:, None, :], w_q=w_q.astype(BF16),
        q_norm=jnp.tile(q_norm, (1, LANES // HEAD_DIM))[:, None, :],
        w_o=w_o.astype(BF16), norm_ffn=norm_ffn[:, None, :],
        w_up=w_ffn_up.astype(BF16), ffn_conv_w=ffn_conv_w, ffn_conv_b=ffn_conv_b[:, None, :],
        w_down=w_ffn_down.astype(BF16),
        hsum=jnp.asarray((head_of[:, None] == head_of[None, :]) / HEAD_DIM, BF16),
    )

    def attend_prompt(q, k, v):
        res = [_attn_prompt(q, k, v, g) for g in range(N_GROUPS)]
        return tuple(o for o, _ in res) + tuple(l for _, l in res)

    y_prompt, k_p, v_p, p_h, p_conv, p_ffn = _trunk(
        x_prompt, jnp.arange(tp),
        jnp.zeros((n_a, bp, CONV_A_WIDTH - 1, d), F32), jnp.zeros((n_a, bp, 1, d), F32),
        jnp.zeros((depth, bp, FFN_CONV_WIDTH - 1, dff), F32), w, attend_prompt, tstride=1)
    keep = min(WINDOWS[-1], tp)
    p_cache_k = k_p[:, tp - keep:].reshape(bp, keep, N_KV_HEADS, HEAD_DIM)
    p_cache_v = v_p[:, tp - keep:].reshape(bp, keep, N_KV_HEADS, HEAD_DIM)

    def to_tm(a):
        a = jnp.swapaxes(a, -3, -2)
        return a.reshape(a.shape[:-3] + (1, a.shape[-3] * bs, a.shape[-1]))

    def from_tm(a):
        a = a.reshape(a.shape[:-3] + (a.shape[-2] // bs, bs, a.shape[-1]))
        return jnp.swapaxes(a, -3, -2)

    def attend_sample(q, k, v):
        o = _attn_sample(from_tm(q), from_tm(k), from_tm(v), cache_k, cache_v)
        return (to_tm(o),)

    pos_s = PAST_LEN + jnp.repeat(jnp.arange(ts), bs)
    y_s, k_s, v_s, s_h, s_conv, s_ffn = _trunk(
        to_tm(x_sample), pos_s, to_tm(state_rglru_conv), state_rglru_h[:, None], to_tm(state_ffn_conv),
        w, attend_sample, tstride=bs)

    return (y_prompt, from_tm(y_s), p_h[:, :, 0, :], p_conv, p_ffn, p_cache_k, p_cache_v,
            s_h[:, 0], from_tm(s_conv), from_tm(s_ffn),
            from_tm(k_s).reshape(bs, ts, N_KV_HEADS, HEAD_DIM),
            from_tm(v_s).reshape(bs, ts, N_KV_HEADS, HEAD_DIM))
```

```python
import functools

import numpy as np
import jax
import jax.numpy as jnp
from jax import lax
from jax.experimental import pallas as pl
from jax.experimental.pallas import tpu as pltpu

F32 = jnp.float32
BF16 = jnp.bfloat16

N_LRU_BLOCKS = 4
LRU_C = 8.0
CONV_A_WIDTH = 4
FFN_CONV_WIDTH = 3
HEAD_DIM = 64
N_KV_HEADS = 8
KV_DIM = N_KV_HEADS * HEAD_DIM
WINDOWS = (128, 512, 2048)
DILATIONS = (1, 4, 16)
N_GROUPS = len(WINDOWS)
SPAN = 128
ROT_DIM = HEAD_DIM // 4
ROPE_THETA = 500000.0
PAST_LEN = 2048
EPS = 1e-6
SCALE = HEAD_DIM ** -0.5

LANES = 128
SUBLANES = 8
N_PAIRS = KV_DIM // LANES
ATT_TILE = 1024
KEY_ROWS = 2 * SPAN
ROW_TILE = 512
FF_CHUNK = 1024
VMEM_LIMIT = 56 * 1024 * 1024

_NT = (((1,), (1,)), ((), ()))


def _rmsnorm(x, g):
    ms = jnp.mean(x * x, axis=-1, keepdims=True)
    return (x * lax.rsqrt(ms + EPS)) * g


def _dot(a, b):
    return jnp.dot(a, b, preferred_element_type=F32)


def _const_spec(shape, single_buffer=False):
    nd = len(shape)
    kw = {"pipeline_mode": pl.Buffered(1)} if single_buffer else {}
    return pl.BlockSpec(shape, lambda *_: (0,) * nd, **kw)


def _params(*sem):
    return pltpu.CompilerParams(dimension_semantics=sem, vmem_limit_bytes=VMEM_LIMIT)


def _conv_pad(width, tstride):
    hist = (width - 1) * tstride
    return hist, max(SUBLANES, hist)


def _causal_conv(cat_ref, cols, w_ref, b_ref, width, tstride, rows):
    hist, pad = _conv_pad(width, tstride)
    acc = None
    for j in range(width):
        start = pad - hist + j * tstride
        term = cat_ref[start:start + rows, cols] * w_ref[j:j + 1, cols]
        acc = term if acc is None else acc + term
    return b_ref[:, cols] + acc


def _mixer_a_kernel(x_ref, cprev_ref, h0_ref, g_ref, win_ref, cw_ref, cb_ref, gw_ref, gb_ref,
                    lam_ref, wout_ref, xo_ref, cst_ref, hl_ref, cat_s, hc_s, a_s, u_s,
                    *, tstride, rows):
    d = x_ref.shape[-1]
    blk = d // N_LRU_BLOCKS
    hist, pad = _conv_pad(CONV_A_WIDTH, tstride)

    @pl.when(pl.program_id(1) == 0)
    def _():
        cat_s[pad - hist:pad, :] = cprev_ref[...]
        hc_s[...] = h0_ref[...]

    x = x_ref[...]
    hn = _rmsnorm(x, g_ref[...]).astype(BF16)
    y = _dot(hn, win_ref[...])
    gate = y[:, :d]
    cat_s[pad:pad + rows, :] = y[:, d:]
    xc = _causal_conv(cat_s, slice(None), cw_ref, cb_ref, CONV_A_WIDTH, tstride, rows)
    new_hist = cat_s[pad + rows - hist:pad + rows, :]
    cat_s[pad - hist:pad, :] = new_hist
    cst_ref[...] = new_hist

    lam = lam_ref[...]
    nsp = -lam
    softplus = jnp.maximum(nsp, 0.0) + jnp.log1p(jnp.exp(-jnp.abs(nsp)))
    xcb = xc.astype(BF16)
    for n in range(N_LRU_BLOCKS):
        cols = slice(n * blk, (n + 1) * blk)
        ri = _dot(xcb[:, cols], gw_ref[n]) + gb_ref[n]
        r = jax.nn.sigmoid(ri[:, :blk])
        i = jax.nn.sigmoid(ri[:, blk:])
        log_a = (-LRU_C) * r * softplus[:, cols]
        a = jnp.exp(log_a)
        a_s[:, cols] = a
        u_s[:, cols] = jnp.sqrt(1.0 - a * a) * (i * xc[:, cols])

    if tstride == 1:
        groups = rows // SUBLANES
        a3 = a_s[...].reshape(groups, SUBLANES, d)
        u3 = u_s[...].reshape(groups, SUBLANES, d)
        sub = lax.broadcasted_iota(jnp.int32, a3.shape, 1)
        for k in (1, 2, 4):
            keep = sub >= k
            u3 = jnp.where(keep, a3 * pltpu.roll(u3, k, 1) + u3, u3)
            a3 = jnp.where(keep, a3 * pltpu.roll(a3, k, 1), a3)
        a_s[...] = a3.reshape(rows, d)
        u_s[...] = u3.reshape(rows, d)

        def carry(gi, hb):
            r0 = pl.multiple_of(gi * SUBLANES, SUBLANES)
            h = a_s[pl.ds(r0, SUBLANES), :] * hb + u_s[pl.ds(r0, SUBLANES), :]
            u_s[pl.ds(r0, SUBLANES), :] = h
            return jnp.broadcast_to(h[SUBLANES - 1:SUBLANES, :], (SUBLANES, d))

        hb = lax.fori_loop(0, groups, carry, jnp.broadcast_to(hc_s[...], (SUBLANES, d)))
        h_last = hb[0:1, :]
    else:
        h_last = hc_s[...]
        for t in range(rows // tstride):
            sl = slice(t * tstride, (t + 1) * tstride)
            h_last = a_s[sl, :] * h_last + u_s[sl, :]
            u_s[sl, :] = h_last
    hc_s[...] = h_last
    hl_ref[...] = h_last

    yv = (u_s[...] * jax.nn.gelu(gate)).astype(BF16)
    xo_ref[...] = x + _dot(yv, wout_ref[...])


def _mixer_a(x, cprev, h0, g, win, cw, cb, gw, gb, lam, wout, *, tstride):
    nb, t, d = x.shape
    rows = min(ROW_TILE, t)
    hist, pad = _conv_pad(CONV_A_WIDTH, tstride)
    row_spec = pl.BlockSpec((None, rows, d), lambda b, i: (b, i, 0))
    per_b = lambda n: pl.BlockSpec((None, n, d), lambda b, i: (b, 0, 0))
    return pl.pallas_call(
        functools.partial(_mixer_a_kernel, tstride=tstride, rows=rows),
        grid=(nb, t // rows),
        in_specs=[row_spec, per_b(hist), per_b(tstride), _const_spec(g.shape),
                  _const_spec(win.shape, True), _const_spec(cw.shape), _const_spec(cb.shape),
                  _const_spec(gw.shape, True), _const_spec(gb.shape), _const_spec(lam.shape),
                  _const_spec(wout.shape, True)],
        out_specs=[row_spec, per_b(hist), per_b(tstride)],
        out_shape=[jax.ShapeDtypeStruct(x.shape, F32),
                   jax.ShapeDtypeStruct((nb, hist, d), F32),
                   jax.ShapeDtypeStruct((nb, tstride, d), F32)],
        scratch_shapes=[pltpu.VMEM((pad + rows, d), F32), pltpu.VMEM((tstride, d), F32),
                        pltpu.VMEM((rows, d), F32), pltpu.VMEM((rows, d), F32)],
        compiler_params=_params("parallel", "arbitrary"),
        name="mixer_a",
    )(x, cprev, h0, g, win, cw, cb, gw, gb, lam, wout)


def _ffn_kernel(*refs, tstride, rows, has_att):
    x_ref = refs[0]
    if has_att:
        att_ref, wo_ref = refs[1:3]
    prev_ref, g_ref, wup_ref, cw_ref, cb_ref, wdn_ref, xo_ref, st_ref, cat_s = refs[1 + 2 * has_att:]
    dff = wdn_ref.shape[0]
    hist, pad = _conv_pad(FFN_CONV_WIDTH, tstride)

    @pl.when(pl.program_id(1) == 0)
    def _():
        cat_s[pad - hist:pad, :] = prev_ref[...]

    x = x_ref[...]
    if has_att:
        merged = jnp.concatenate([att_ref[p] for p in range(att_ref.shape[0])], axis=1)
        x = x + _dot(merged.astype(BF16), wo_ref[...])

    hn = _rmsnorm(x, g_ref[...]).astype(BF16)
    acc = None
    for c in range(dff // FF_CHUNK):
        cols = slice(c * FF_CHUNK, (c + 1) * FF_CHUNK)
        vcols = slice(dff + c * FF_CHUNK, dff + (c + 1) * FF_CHUNK)
        cat_s[pad:pad + rows, cols] = _dot(hn, wup_ref[:, cols])
        gc = _causal_conv(cat_s, cols, cw_ref, cb_ref, FFN_CONV_WIDTH, tstride, rows)
        act = (jax.nn.gelu(gc) * _dot(hn, wup_ref[:, vcols])).astype(BF16)
        part = _dot(act, wdn_ref[cols, :])
        acc = part if acc is None else acc + part
    new_hist = cat_s[pad + rows - hist:pad + rows, :]
    cat_s[pad - hist:pad, :] = new_hist
    st_ref[...] = new_hist
    xo_ref[...] = x + acc


def _conv_ffn(x, att, wo, prev, g, wup, cw, cb, wdn, *, tstride):
    nb, t, d = x.shape
    dff = wdn.shape[0]
    rows = min(ROW_TILE, t)
    hist, pad = _conv_pad(FFN_CONV_WIDTH, tstride)
    row_spec = pl.BlockSpec((None, rows, d), lambda b, i: (b, i, 0))
    st_spec = pl.BlockSpec((None, hist, dff), lambda b, i: (b, 0, 0))
    has_att = att is not None
    att_args, att_specs = (), []
    if has_att:
        att_args = (att, wo)
        att_specs = [pl.BlockSpec((None, N_PAIRS, rows, LANES), lambda b, i: (b, 0, i, 0)),
                     _const_spec(wo.shape)]
    return pl.pallas_call(
        functools.partial(_ffn_kernel, tstride=tstride, rows=rows, has_att=has_att),
        grid=(nb, t // rows),
        in_specs=[row_spec] + att_specs
        + [st_spec, _const_spec(g.shape), _const_spec(wup.shape, True), _const_spec(cw.shape),
           _const_spec(cb.shape), _const_spec(wdn.shape, True)],
        out_specs=[row_spec, st_spec],
        out_shape=[jax.ShapeDtypeStruct(x.shape, F32), jax.ShapeDtypeStruct((nb, hist, dff), F32)],
        scratch_shapes=[pltpu.VMEM((pad + rows, dff), F32)],
        compiler_params=_params("parallel", "arbitrary"),
        name="conv_ffn",
    )(x, *att_args, prev, g, wup, cw, cb, wdn)


def _proj_kernel(x_ref, g_ref, w_ref, hsum_ref, gh_ref, cos_ref, sin_ref, *out_refs, n_rope):
    hn = _rmsnorm(x_ref[...], g_ref[...]).astype(BF16)
    y = _dot(hn, w_ref[...])
    rows = y.shape[0]
    lane = lax.broadcasted_iota(jnp.int32, (rows, LANES), 1)
    first_half = (lane & (HEAD_DIM - 1)) < (ROT_DIM // 2)
    hsum = hsum_ref[...]
    cos = cos_ref[...]
    sin = sin_ref[...]
    for i in range(n_rope // LANES):
        cols = slice(i * LANES, (i + 1) * LANES)
        yt = y[:, cols]
        sq = yt * yt
        hi = sq.astype(BF16)
        lo = (sq - hi.astype(F32)).astype(BF16)
        ms = _dot(hi, hsum) + _dot(lo, hsum)
        yn = (yt * lax.rsqrt(ms + EPS)) * gh_ref[...]
        partner = jnp.where(first_half, pltpu.roll(yn, LANES - ROT_DIM // 2, 1),
                            pltpu.roll(yn, ROT_DIM // 2, 1))
        out_refs[0][i] = yn * cos + partner * sin
    if len(out_refs) > 1:
        for i in range((y.shape[1] - n_rope) // LANES):
            out_refs[1][i] = y[:, n_rope + i * LANES:n_rope + (i + 1) * LANES]


def _project(x, g, w, hsum, gh, cos, sin, *, n_rope):
    nb, t, d = x.shape
    n = w.shape[1]
    rows = min(ROW_TILE, t)
    row_in = pl.BlockSpec((None, rows, d), lambda b, i: (b, i, 0))
    tab = pl.BlockSpec((rows, LANES), lambda b, i: (i, 0))
    tiles = [n_rope // LANES] + ([(n - n_rope) // LANES] if n > n_rope else [])
    return pl.pallas_call(
        functools.partial(_proj_kernel, n_rope=n_rope),
        grid=(nb, t // rows),
        in_specs=[row_in, _const_spec(g.shape), _const_spec(w.shape), _const_spec(hsum.shape),
                  _const_spec(gh.shape), tab, tab],
        out_specs=[pl.BlockSpec((None, nt, rows, LANES), lambda b, i: (b, 0, i, 0)) for nt in tiles],
        out_shape=[jax.ShapeDtypeStruct((nb, nt, t, LANES), F32) for nt in tiles],
        compiler_params=_params("parallel", "parallel"),
        name="norm_proj_rope",
    )(x, g, w, hsum, gh, cos, sin)


def _attn_group(q_ref, k_ref, v_ref, og_s, lg_s, tile_start, dil, tile):
    per_res = tile // dil
    nq = min(SPAN, per_res)
    nsub = per_res // nq
    sub0 = tile_start // dil
    qi = lax.broadcasted_iota(jnp.int32, (2 * nq, KEY_ROWS), 0) & (nq - 1)
    kj = lax.broadcasted_iota(jnp.int32, (2 * nq, KEY_ROWS), 1)
    rel = qi - kj
    low = lax.broadcasted_iota(jnp.int32, (nq, LANES), 1) < HEAD_DIM

    def rows(start, n):
        return pl.ds(start, n, stride=dil) if dil > 1 else pl.ds(start, n)

    def block(idx, carry):
        r = idx // nsub
        j = idx - r * nsub
        q0 = sub0 + j * nq
        k0 = jnp.maximum(q0 + nq - KEY_ROWS, 0)
        dist = rel + (q0 - k0)
        valid = (dist >= 0) & (dist <= SPAN)
        q_rows = rows(r + dil * (j * nq), nq)
        k_rows = rows(r + dil * k0, KEY_ROWS)
        for p in range(N_PAIRS):
            qp = (q_ref[p, q_rows, :] * SCALE).astype(BF16)
            zero = jnp.zeros_like(qp)
            qs = jnp.concatenate([jnp.where(low, qp, zero), jnp.where(low, zero, qp)], axis=0)
            s = lax.dot_general(qs, k_ref[p, k_rows, :].astype(BF16), _NT, preferred_element_type=F32)
            s = jnp.where(valid, s, -jnp.inf)
            m = jnp.max(s, axis=-1, keepdims=True)
            pe = jnp.exp(s - m)
            den = jnp.sum(pe, axis=-1, keepdims=True)
            o = _dot(pe.astype(BF16), v_ref[p, k_rows, :].astype(BF16)) / den
            lse = jnp.broadcast_to(m + jnp.log(den), o.shape)
            og_s[p, q_rows, :] = jnp.where(low, o[:nq], o[nq:])
            lg_s[p, q_rows, :] = jnp.where(low, lse[:nq], lse[nq:])
        return carry

    lax.fori_loop(0, dil * nsub, block, 0)


def _attn_prompt_kernel(q_ref, k_ref, v_ref, o_ref, og_s, lg_s, oa_s, la_s, *, tile):
    group = pl.program_id(2)
    tile_start = pl.program_id(1) * tile
    for gi, dil in enumerate(DILATIONS):
        @pl.when(group == gi)
        def _(dil=dil):
            _attn_group(q_ref, k_ref, v_ref, og_s, lg_s, tile_start, dil, tile)

    @pl.when(group == 0)
    def _():
        oa_s[...] = og_s[...]
        la_s[...] = lg_s[...]

    @pl.when(group > 0)
    def _():
        for p in range(N_PAIRS):
            la, lg = la_s[p], lg_s[p]
            top = jnp.maximum(la, lg)
            ea, eg = jnp.exp(la - top), jnp.exp(lg - top)
            tot = ea + eg
            oa_s[p] = (ea * oa_s[p] + eg * og_s[p]) / tot
            la_s[p] = top + jnp.log(tot)

    @pl.when(group == N_GROUPS - 1)
    def _():
        o_ref[...] = oa_s[...]


def _attn_prompt(q, k, v):
    nb, _, t, _ = k.shape
    tile = min(ATT_TILE, t)
    assert tile % (DILATIONS[-1] * SUBLANES) == 0 and t % tile == 0 and t >= DILATIONS[-1] * KEY_ROWS
    whole = pl.BlockSpec((None, N_PAIRS, t, LANES), lambda b, i, g: (b, 0, 0, 0),
                         pipeline_mode=pl.Buffered(1))
    scratch = pltpu.VMEM((N_PAIRS, tile, LANES), F32)
    return pl.pallas_call(
        functools.partial(_attn_prompt_kernel, tile=tile),
        grid=(nb, t // tile, N_GROUPS),
        in_specs=[pl.BlockSpec((None, N_PAIRS, tile, LANES), lambda b, i, g: (b, g, i, 0)), whole, whole],
        out_specs=pl.BlockSpec((None, N_PAIRS, tile, LANES), lambda b, i, g: (b, 0, i, 0)),
        out_shape=jax.ShapeDtypeStruct(k.shape, F32),
        scratch_shapes=[scratch] * 4,
        compiler_params=_params("parallel", "arbitrary", "arbitrary"),
        name="attn_prompt",
    )(q, k, v)


def _attn_sample_kernel(q_ref, kn_ref, vn_ref, k1_ref, k2_ref, v1_ref, v2_ref, o_ref, kn_s, vn_s):
    steps = q_ref.shape[0]
    nrow = steps * N_KV_HEADS
    kn_s[...] = jnp.zeros_like(kn_s)
    vn_s[...] = jnp.zeros_like(vn_s)
    for s in range(steps):
        kn_s[s * N_KV_HEADS:(s + 1) * N_KV_HEADS, :] = kn_ref[s]
        vn_s[s * N_KV_HEADS:(s + 1) * N_KV_HEADS, :] = vn_ref[s]

    def flat(a):
        return a.reshape(-1, HEAD_DIM).astype(BF16)

    def q_rows(g):
        parts = [q_ref[s, g * N_KV_HEADS:(g + 1) * N_KV_HEADS, :] for s in range(steps)]
        return (jnp.concatenate(parts, axis=0) * SCALE).astype(BF16)

    def scores(qm, km):
        return lax.dot_general(qm, km, _NT, preferred_element_type=F32)

    def coords(width):
        row = lax.broadcasted_iota(jnp.int32, (nrow, width), 0)
        return row >> 3, row & (N_KV_HEADS - 1), lax.broadcasted_iota(jnp.int32, (nrow, width), 1)

    k1, k2, kn = flat(k1_ref[...]), flat(k2_ref[...]), flat(kn_s[...])
    v1, v2, vn = flat(v1_ref[...]), flat(v2_ref[...]), flat(vn_s[...])
    n1 = k1.shape[0]
    n0 = SPAN * N_KV_HEADS
    qg = [q_rows(g) for g in range(N_GROUPS)]
    neg = -jnp.inf

    srow, hrow, c = coords(n0)
    s0 = jnp.where(((c & (N_KV_HEADS - 1)) == hrow) & ((c >> 3) >= srow), scores(qg[0], k1[n1 - n0:]), neg)
    srow, hrow, c = coords(n1)
    own = ((c & (N_KV_HEADS - 1)) == hrow) & (((c >> 3) & (steps - 1)) == srow)
    s1 = jnp.where(own, scores(qg[1], k1), neg)
    s2 = jnp.where(own, scores(qg[2], k2), neg)
    srow, hrow, c = coords(SPAN)
    same = ((c & (N_KV_HEADS - 1)) == hrow) & (c < nrow)
    new_ok = [same & (((c >> 3) <= srow) if dil == 1 else ((c >> 3) == srow)) for dil in DILATIONS]
    s_new = [jnp.where(new_ok[g], scores(qg[g], kn), neg) for g in range(N_GROUPS)]

    parts = [s0, s1, s2] + s_new
    top = functools.reduce(jnp.maximum, [jnp.max(s, axis=-1, keepdims=True) for s in parts])
    p0, p1, p2 = [jnp.exp(s - top) for s in (s0, s1, s2)]
    p_new = functools.reduce(jnp.add, [jnp.exp(s - top) for s in s_new])
    den = functools.reduce(jnp.add, [jnp.sum(p, axis=-1, keepdims=True) for p in (p0, p1, p2, p_new)])

    p01 = jnp.concatenate([p1[:, :n1 - n0], p1[:, n1 - n0:] + p0], axis=1)
    acc = _dot(p01.astype(BF16), v1) + _dot(p2.astype(BF16), v2) + _dot(p_new.astype(BF16), vn)
    out = acc / den
    for s in range(steps):
        o_ref[s] = out[s * N_KV_HEADS:(s + 1) * N_KV_HEADS, :]


def _attn_sample(q, kn, vn, cache_k, cache_v):
    nb, steps, heads, hd = kn.shape
    past = cache_k.shape[1]
    d1, d2 = DILATIONS[1], DILATIONS[2]
    assert past == WINDOWS[-1] and steps == d1 and PAST_LEN % d2 == 0, (past, steps)
    recent = pl.BlockSpec((None, SPAN * d1, heads, hd), lambda b: (b, past // (SPAN * d1) - 1, 0, 0))
    strided = pl.BlockSpec((None, past // d2, steps, heads, hd), lambda b: (b, 0, 0, 0, 0))
    views = [a for c in (cache_k, cache_v) for a in (c, c.reshape(nb, past // d2, d2, heads, hd))]
    specs = [recent, strided] * 2
    new_spec = pl.BlockSpec((None, steps, heads, hd), lambda b: (b, 0, 0, 0))
    return pl.pallas_call(
        _attn_sample_kernel,
        grid=(nb,),
        in_specs=[pl.BlockSpec((None, steps, N_GROUPS * heads, hd), lambda b: (b, 0, 0, 0)),
                  new_spec, new_spec] + specs,
        out_specs=new_spec,
        out_shape=jax.ShapeDtypeStruct(kn.shape, F32),
        scratch_shapes=[pltpu.VMEM((SPAN, hd), F32), pltpu.VMEM((SPAN, hd), F32)],
        compiler_params=_params("parallel"),
        name="attn_sample",
    )(q, kn, vn, *views)


def _rope_tables(pos):
    half = ROT_DIM // 2
    inv = ROPE_THETA ** (-jnp.arange(half, dtype=F32) * (2.0 / ROT_DIM))
    dlane = np.arange(LANES) % HEAD_DIM
    inv_lane = jnp.where(dlane < ROT_DIM, inv[dlane % half], 0.0)
    ang = pos.astype(F32)[:, None] * inv_lane[None, :]
    sign = np.where(dlane < half, -1.0, 1.0).astype(np.float32)
    return jnp.cos(ang), jnp.sin(ang) * sign


def _trunk(x, pos, conv_a_prev, h0, ffn_prev, w, attend, *, tstride):
    n_a = w["w_in"].shape[0]
    depth = w["w_up"].shape[0]
    cos, sin = _rope_tables(pos)
    hs_new, conv_new, ffn_new = [], [], []
    k = v = None
    for layer in range(depth):
        att = wo = None
        if layer < n_a:
            x, cs, hl = _mixer_a(x, conv_a_prev[layer], h0[layer], w["norm_a"][layer], w["w_in"][layer],
                                 w["conv_a_w"][layer], w["conv_a_b"][layer], w["gate_w"][layer],
                                 w["gate_b"][layer], w["lam"][layer], w["w_out"][layer], tstride=tstride)
            conv_new.append(cs)
            hs_new.append(hl)
        else:
            j = layer - n_a
            (q,) = _project(x, w["norm_b"][j], w["w_q"][j], w["hsum"], w["q_norm"][j], cos, sin,
                            n_rope=w["w_q"].shape[-1])
            att, wo = attend(q, k, v), w["w_o"][j]
        x, fs = _conv_ffn(x, att, wo, ffn_prev[layer], w["norm_ffn"][layer], w["w_up"][layer],
                          w["ffn_conv_w"][layer], w["ffn_conv_b"][layer], w["w_down"][layer],
                          tstride=tstride)
        ffn_new.append(fs)
        if layer == n_a - 1:
            k, v = _project(x, w["norm_kv"], w["w_kv"], w["hsum"], w["k_norm"], cos, sin, n_rope=KV_DIM)
    return x, k, v, jnp.stack(hs_new), jnp.stack(conv_new), jnp.stack(ffn_new)


def kernel(x_prompt, x_sample, cache_k, cache_v, state_rglru_h, state_rglru_conv, state_ffn_conv, norm_mix_a, w_in_a, conv_a_w, conv_a_b, gate_r_w, gate_r_b, gate_i_w, gate_i_b, lru_lambda, w_out_a, norm_kv, w_kv, k_norm, norm_mix_b, w_q, q_norm, w_o, norm_ffn, w_ffn_up, ffn_conv_w, ffn_conv_b, w_ffn_down):
    n_a, d = norm_mix_a.shape
    depth, _, dff2 = w_ffn_up.shape
    dff = dff2 // 2
    bp, tp, _ = x_prompt.shape
    bs, ts, _ = x_sample.shape

    head_of = np.arange(LANES) // HEAD_DIM
    w = dict(
        norm_a=norm_mix_a[:, None, :], w_in=w_in_a.astype(BF16),
        conv_a_w=conv_a_w, conv_a_b=conv_a_b[:, None, :],
        gate_w=jnp.concatenate([gate_r_w, gate_i_w], axis=-1).astype(BF16),
        gate_b=jnp.concatenate([gate_r_b, gate_i_b], axis=-1)[:, :, None, :],
        lam=lru_lambda[:, None, :], w_out=w_out_a.astype(BF16),
        norm_kv=norm_kv[None, :], w_kv=w_kv.astype(BF16),
        k_norm=jnp.tile(k_norm, LANES // HEAD_DIM)[None, :],
        norm_b=norm_mix_b[:, None, :], w_q=w_q.astype(BF16),
        q_norm=jnp.tile(q_norm, (1, LANES // HEAD_DIM))[:, None, :],
        w_o=w_o.astype(BF16), norm_ffn=norm_ffn[:, None, :],
        w_up=w_ffn_up.astype(BF16), ffn_conv_w=ffn_conv_w, ffn_conv_b=ffn_conv_b[:, None, :],
        w_down=w_ffn_down.astype(BF16),
        hsum=jnp.asarray((head_of[:, None] == head_of[None, :]) / HEAD_DIM, BF16),
    )

    y_prompt, k_p, v_p, p_h, p_conv, p_ffn = _trunk(
        x_prompt, jnp.arange(tp),
        jnp.zeros((n_a, bp, CONV_A_WIDTH - 1, d), F32), jnp.zeros((n_a, bp, 1, d), F32),
        jnp.zeros((depth, bp, FFN_CONV_WIDTH - 1, dff), F32), w, _attn_prompt, tstride=1)
    keep = min(WINDOWS[-1], tp)

    def last_rows(a):
        return jnp.swapaxes(a[:, :, tp - keep:], 1, 2).reshape(bp, keep, N_KV_HEADS, HEAD_DIM)

    p_cache_k, p_cache_v = last_rows(k_p), last_rows(v_p)

    def to_tm(a):
        a = jnp.swapaxes(a, -3, -2)
        return a.reshape(a.shape[:-3] + (1, a.shape[-3] * bs, a.shape[-1]))

    def from_tm(a):
        a = a.reshape(a.shape[:-3] + (a.shape[-2] // bs, bs, a.shape[-1]))
        return jnp.swapaxes(a, -3, -2)

    def to_heads(a):
        a = a.reshape(a.shape[1], ts, bs, LANES // HEAD_DIM, HEAD_DIM)
        return jnp.transpose(a, (2, 1, 0, 3, 4)).reshape(bs, ts, -1, HEAD_DIM)

    def from_heads(a):
        a = a.reshape(bs, ts, -1, LANES)
        return jnp.transpose(a, (2, 1, 0, 3)).reshape(1, -1, ts * bs, LANES)

    def attend_sample(q, k, v):
        return from_heads(_attn_sample(to_heads(q), to_heads(k), to_heads(v), cache_k, cache_v))

    pos_s = PAST_LEN + jnp.repeat(jnp.arange(ts), bs)
    y_s, k_s, v_s, s_h, s_conv, s_ffn = _trunk(
        to_tm(x_sample), pos_s, to_tm(state_rglru_conv), state_rglru_h[:, None], to_tm(state_ffn_conv),
        w, attend_sample, tstride=bs)

    return (y_prompt, from_tm(y_s), p_h[:, :, 0, :], p_conv, p_ffn, p_cache_k, p_cache_v,
            s_h[:, 0], from_tm(s_conv), from_tm(s_ffn), to_heads(k_s), to_heads(v_s))
```

```python
import functools

import numpy as np
import jax
import jax.numpy as jnp
from jax import lax
from jax.experimental import pallas as pl
from jax.experimental.pallas import tpu as pltpu

F32 = jnp.float32
BF16 = jnp.bfloat16

N_LRU_BLOCKS = 4
LRU_C = 8.0
CONV_A_WIDTH = 4
FFN_CONV_WIDTH = 3
HEAD_DIM = 64
N_KV_HEADS = 8
KV_DIM = N_KV_HEADS * HEAD_DIM
WINDOWS = (128, 512, 2048)
DILATIONS = (1, 4, 16)
N_GROUPS = len(WINDOWS)
SPAN = 128
ROT_DIM = HEAD_DIM // 4
ROPE_THETA = 500000.0
PAST_LEN = 2048
EPS = 1e-6
SCALE = HEAD_DIM ** -0.5

LANES = 128
SUBLANES = 8
N_PAIRS = KV_DIM // LANES
ATT_TILE = 2048
MERGE_ROWS = 128
LOG2E = 1.4426950408889634
LN2 = 0.6931471805599453
KEY_ROWS = 2 * SPAN
ROW_TILE = 512
FF_CHUNK = 1024
VMEM_LIMIT = 56 * 1024 * 1024

_NT = (((1,), (1,)), ((), ()))


def _rmsnorm(x, g):
    ms = jnp.mean(x * x, axis=-1, keepdims=True)
    return (x * lax.rsqrt(ms + EPS)) * g


def _dot(a, b):
    return jnp.dot(a, b, preferred_element_type=F32)


def _const_spec(shape, single_buffer=False):
    nd = len(shape)
    kw = {"pipeline_mode": pl.Buffered(1)} if single_buffer else {}
    return pl.BlockSpec(shape, lambda *_: (0,) * nd, **kw)


def _params(*sem):
    return pltpu.CompilerParams(dimension_semantics=sem, vmem_limit_bytes=VMEM_LIMIT)


def _conv_pad(width, tstride):
    hist = (width - 1) * tstride
    return hist, max(SUBLANES, hist)


def _causal_conv(cat_ref, cols, w_ref, b_ref, width, tstride, rows):
    hist, pad = _conv_pad(width, tstride)
    acc = None
    for j in range(width):
        start = pad - hist + j * tstride
        term = cat_ref[start:start + rows, cols] * w_ref[j:j + 1, cols]
        acc = term if acc is None else acc + term
    return b_ref[:, cols] + acc


def _mixer_a_kernel(x_ref, cprev_ref, h0_ref, g_ref, win_ref, cw_ref, cb_ref, gw_ref, gb_ref,
                    lam_ref, wout_ref, xo_ref, cst_ref, hl_ref, cat_s, hc_s, a_s, u_s,
                    *, tstride, rows):
    d = x_ref.shape[-1]
    blk = d // N_LRU_BLOCKS
    hist, pad = _conv_pad(CONV_A_WIDTH, tstride)

    @pl.when(pl.program_id(1) == 0)
    def _():
        cat_s[pad - hist:pad, :] = cprev_ref[...]
        hc_s[...] = h0_ref[...]

    x = x_ref[...]
    hn = _rmsnorm(x, g_ref[...]).astype(BF16)
    y = _dot(hn, win_ref[...])
    gate = y[:, :d]
    cat_s[pad:pad + rows, :] = y[:, d:]
    xc = _causal_conv(cat_s, slice(None), cw_ref, cb_ref, CONV_A_WIDTH, tstride, rows)
    new_hist = cat_s[pad + rows - hist:pad + rows, :]
    cat_s[pad - hist:pad, :] = new_hist
    cst_ref[...] = new_hist

    lam = lam_ref[...]
    nsp = -lam
    softplus = jnp.maximum(nsp, 0.0) + jnp.log1p(jnp.exp(-jnp.abs(nsp)))
    xcb = xc.astype(BF16)
    for n in range(N_LRU_BLOCKS):
        cols = slice(n * blk, (n + 1) * blk)
        ri = _dot(xcb[:, cols], gw_ref[n]) + gb_ref[n]
        r = jax.nn.sigmoid(ri[:, :blk])
        i = jax.nn.sigmoid(ri[:, blk:])
        log_a = (-LRU_C) * r * softplus[:, cols]
        a = jnp.exp(log_a)
        a_s[:, cols] = a
        u_s[:, cols] = jnp.sqrt(1.0 - a * a) * (i * xc[:, cols])

    if tstride == 1:
        groups = rows // SUBLANES
        a3 = a_s[...].reshape(groups, SUBLANES, d)
        u3 = u_s[...].reshape(groups, SUBLANES, d)
        sub = lax.broadcasted_iota(jnp.int32, a3.shape, 1)
        for k in (1, 2, 4):
            keep = sub >= k
            u3 = jnp.where(keep, a3 * pltpu.roll(u3, k, 1) + u3, u3)
            a3 = jnp.where(keep, a3 * pltpu.roll(a3, k, 1), a3)
        a_s[...] = a3.reshape(rows, d)
        u_s[...] = u3.reshape(rows, d)

        def carry(gi, hb):
            r0 = pl.multiple_of(gi * SUBLANES, SUBLANES)
            h = a_s[pl.ds(r0, SUBLANES), :] * hb + u_s[pl.ds(r0, SUBLANES), :]
            u_s[pl.ds(r0, SUBLANES), :] = h
            return jnp.broadcast_to(h[SUBLANES - 1:SUBLANES, :], (SUBLANES, d))

        hb = lax.fori_loop(0, groups, carry, jnp.broadcast_to(hc_s[...], (SUBLANES, d)))
        h_last = hb[0:1, :]
    else:
        h_last = hc_s[...]
        for t in range(rows // tstride):
            sl = slice(t * tstride, (t + 1) * tstride)
            h_last = a_s[sl, :] * h_last + u_s[sl, :]
            u_s[sl, :] = h_last
    hc_s[...] = h_last
    hl_ref[...] = h_last

    yv = (u_s[...] * jax.nn.gelu(gate)).astype(BF16)
    xo_ref[...] = x + _dot(yv, wout_ref[...])


def _mixer_a(x, cprev, h0, g, win, cw, cb, gw, gb, lam, wout, *, tstride):
    nb, t, d = x.shape
    rows = min(ROW_TILE, t)
    hist, pad = _conv_pad(CONV_A_WIDTH, tstride)
    row_spec = pl.BlockSpec((None, rows, d), lambda b, i: (b, i, 0))
    per_b = lambda n: pl.BlockSpec((None, n, d), lambda b, i: (b, 0, 0))
    return pl.pallas_call(
        functools.partial(_mixer_a_kernel, tstride=tstride, rows=rows),
        grid=(nb, t // rows),
        in_specs=[row_spec, per_b(hist), per_b(tstride), _const_spec(g.shape),
                  _const_spec(win.shape, True), _const_spec(cw.shape), _const_spec(cb.shape),
                  _const_spec(gw.shape, True), _const_spec(gb.shape), _const_spec(lam.shape),
                  _const_spec(wout.shape, True)],
        out_specs=[row_spec, per_b(hist), per_b(tstride)],
        out_shape=[jax.ShapeDtypeStruct(x.shape, F32),
                   jax.ShapeDtypeStruct((nb, hist, d), F32),
                   jax.ShapeDtypeStruct((nb, tstride, d), F32)],
        scratch_shapes=[pltpu.VMEM((pad + rows, d), F32), pltpu.VMEM((tstride, d), F32),
                        pltpu.VMEM((rows, d), F32), pltpu.VMEM((rows, d), F32)],
        compiler_params=_params("parallel", "arbitrary"),
        name="mixer_a",
    )(x, cprev, h0, g, win, cw, cb, gw, gb, lam, wout)


def _ffn_kernel(*refs, tstride, rows, has_att):
    x_ref = refs[0]
    if has_att:
        att_ref, wo_ref = refs[1:3]
    prev_ref, g_ref, wup_ref, cw_ref, cb_ref, wdn_ref, xo_ref, st_ref, cat_s = refs[1 + 2 * has_att:]
    dff = wdn_ref.shape[0]
    hist, pad = _conv_pad(FFN_CONV_WIDTH, tstride)

    @pl.when(pl.program_id(1) == 0)
    def _():
        cat_s[pad - hist:pad, :] = prev_ref[...]

    x = x_ref[...]
    if has_att:
        merged = jnp.concatenate([att_ref[p] for p in range(att_ref.shape[0])], axis=1)
        x = x + _dot(merged.astype(BF16), wo_ref[...])

    hn = _rmsnorm(x, g_ref[...]).astype(BF16)
    acc = None
    for c in range(dff // FF_CHUNK):
        cols = slice(c * FF_CHUNK, (c + 1) * FF_CHUNK)
        vcols = slice(dff + c * FF_CHUNK, dff + (c + 1) * FF_CHUNK)
        cat_s[pad:pad + rows, cols] = _dot(hn, wup_ref[:, cols])
        gc = _causal_conv(cat_s, cols, cw_ref, cb_ref, FFN_CONV_WIDTH, tstride, rows)
        act = (jax.nn.gelu(gc) * _dot(hn, wup_ref[:, vcols])).astype(BF16)
        part = _dot(act, wdn_ref[cols, :])
        acc = part if acc is None else acc + part
    new_hist = cat_s[pad + rows - hist:pad + rows, :]
    cat_s[pad - hist:pad, :] = new_hist
    st_ref[...] = new_hist
    xo_ref[...] = x + acc


def _conv_ffn(x, att, wo, prev, g, wup, cw, cb, wdn, *, tstride):
    nb, t, d = x.shape
    dff = wdn.shape[0]
    rows = min(ROW_TILE, t)
    hist, pad = _conv_pad(FFN_CONV_WIDTH, tstride)
    row_spec = pl.BlockSpec((None, rows, d), lambda b, i: (b, i, 0))
    st_spec = pl.BlockSpec((None, hist, dff), lambda b, i: (b, 0, 0))
    has_att = att is not None
    att_args, att_specs = (), []
    if has_att:
        att_args = (att, wo)
        att_specs = [pl.BlockSpec((None, N_PAIRS, rows, LANES), lambda b, i: (b, 0, i, 0)),
                     _const_spec(wo.shape)]
    return pl.pallas_call(
        functools.partial(_ffn_kernel, tstride=tstride, rows=rows, has_att=has_att),
        grid=(nb, t // rows),
        in_specs=[row_spec] + att_specs
        + [st_spec, _const_spec(g.shape), _const_spec(wup.shape, True), _const_spec(cw.shape),
           _const_spec(cb.shape), _const_spec(wdn.shape, True)],
        out_specs=[row_spec, st_spec],
        out_shape=[jax.ShapeDtypeStruct(x.shape, F32), jax.ShapeDtypeStruct((nb, hist, dff), F32)],
        scratch_shapes=[pltpu.VMEM((pad + rows, dff), F32)],
        compiler_params=_params("parallel", "arbitrary"),
        name="conv_ffn",
    )(x, *att_args, prev, g, wup, cw, cb, wdn)


def _proj_kernel(x_ref, g_ref, w_ref, hsum_ref, gh_ref, cos_ref, sin_ref, *out_refs, n_rope):
    hn = _rmsnorm(x_ref[...], g_ref[...]).astype(BF16)
    y = _dot(hn, w_ref[...])
    rows = y.shape[0]
    lane = lax.broadcasted_iota(jnp.int32, (rows, LANES), 1)
    first_half = (lane & (HEAD_DIM - 1)) < (ROT_DIM // 2)
    hsum = hsum_ref[...]
    cos = cos_ref[...]
    sin = sin_ref[...]
    for i in range(n_rope // LANES):
        cols = slice(i * LANES, (i + 1) * LANES)
        yt = y[:, cols]
        sq = yt * yt
        hi = sq.astype(BF16)
        lo = (sq - hi.astype(F32)).astype(BF16)
        ms = _dot(hi, hsum) + _dot(lo, hsum)
        yn = (yt * lax.rsqrt(ms + EPS)) * gh_ref[...]
        partner = jnp.where(first_half, pltpu.roll(yn, LANES - ROT_DIM // 2, 1),
                            pltpu.roll(yn, ROT_DIM // 2, 1))
        out_refs[0][i] = yn * cos + partner * sin
    if len(out_refs) > 1:
        for i in range((y.shape[1] - n_rope) // LANES):
            out_refs[1][i] = y[:, n_rope + i * LANES:n_rope + (i + 1) * LANES]


def _project(x, g, w, hsum, gh, cos, sin, *, n_rope):
    nb, t, d = x.shape
    n = w.shape[1]
    rows = min(ROW_TILE, t)
    row_in = pl.BlockSpec((None, rows, d), lambda b, i: (b, i, 0))
    tab = pl.BlockSpec((rows, LANES), lambda b, i: (i, 0))
    tiles = [n_rope // LANES] + ([(n - n_rope) // LANES] if n > n_rope else [])
    return pl.pallas_call(
        functools.partial(_proj_kernel, n_rope=n_rope),
        grid=(nb, t // rows),
        in_specs=[row_in, _const_spec(g.shape), _const_spec(w.shape), _const_spec(hsum.shape),
                  _const_spec(gh.shape), tab, tab],
        out_specs=[pl.BlockSpec((None, nt, rows, LANES), lambda b, i: (b, 0, i, 0)) for nt in tiles],
        out_shape=[jax.ShapeDtypeStruct((nb, nt, t, LANES), F32) for nt in tiles],
        compiler_params=_params("parallel", "parallel"),
        name="norm_proj_rope",
    )(x, g, w, hsum, gh, cos, sin)


def _attn_group(q_ref, k_ref, v_ref, og_s, lg_s, tile_start, dil, tile):
    per_res = tile // dil
    nq = min(SPAN, per_res)
    nsub = per_res // nq
    sub0 = tile_start // dil
    qi = lax.broadcasted_iota(jnp.int32, (2 * nq, KEY_ROWS), 0) & (nq - 1)
    kj = lax.broadcasted_iota(jnp.int32, (2 * nq, KEY_ROWS), 1)
    rel = qi - kj
    low = lax.broadcasted_iota(jnp.int32, (nq, LANES), 1) < HEAD_DIM

    def rows(start, n):
        return pl.ds(start, n, stride=dil) if dil > 1 else pl.ds(start, n)

    def block(idx, carry):
        r = idx // nsub
        j = idx - r * nsub
        q0 = sub0 + j * nq
        k0 = jnp.maximum(q0 + nq - KEY_ROWS, 0)
        dist = rel + (q0 - k0)
        bias = jnp.where((dist >= 0) & (dist <= SPAN), 0.0, -jnp.inf)
        q_rows = rows(r + dil * (j * nq), nq)
        k_rows = rows(r + dil * k0, KEY_ROWS)
        for p in range(N_PAIRS):
            qp = (q_ref[p, q_rows, :] * (SCALE * LOG2E)).astype(BF16)
            zero = jnp.zeros_like(qp)
            qs = jnp.concatenate([jnp.where(low, qp, zero), jnp.where(low, zero, qp)], axis=0)
            s = lax.dot_general(qs, k_ref[p, k_rows, :].astype(BF16), _NT, preferred_element_type=F32)
            s = s + bias
            m = jnp.max(s, axis=-1, keepdims=True)
            pe = jnp.exp2(s - m)
            den = jnp.sum(pe, axis=-1, keepdims=True)
            o = _dot(pe.astype(BF16), v_ref[p, k_rows, :].astype(BF16)) / den
            lse = jnp.broadcast_to(m * LN2 + jnp.log(den), o.shape)
            og_s[p, q_rows, :] = jnp.where(low, o[:nq], o[nq:])
            lg_s[p, q_rows, :] = jnp.where(low, lse[:nq], lse[nq:])
        return carry

    lax.fori_loop(0, dil * nsub, block, 0, unroll=2)


def _attn_prompt_kernel(q_ref, k_ref, v_ref, o_ref, og_s, lg_s, la_s, *, tile):
    group = pl.program_id(2)
    tile_start = pl.program_id(1) * tile
    for gi, dil in enumerate(DILATIONS):
        @pl.when(group == gi)
        def _(gi=gi, dil=dil):
            dst = (o_ref, la_s) if gi == 0 else (og_s, lg_s)
            _attn_group(q_ref, k_ref, v_ref, *dst, tile_start, dil, tile)

    @pl.when(group > 0)
    def _():
        def merge(c, carry):
            rows = pl.ds(pl.multiple_of(c * MERGE_ROWS, MERGE_ROWS), MERGE_ROWS)
            for p in range(N_PAIRS):
                la, lg = la_s[p, rows, :], lg_s[p, rows, :]
                top = jnp.maximum(la, lg)
                ea, eg = jnp.exp(la - top), jnp.exp(lg - top)
                tot = ea + eg
                o_ref[p, rows, :] = (ea * o_ref[p, rows, :] + eg * og_s[p, rows, :]) / tot
                la_s[p, rows, :] = top + jnp.log(tot)
            return carry

        lax.fori_loop(0, tile // MERGE_ROWS, merge, 0)


def _attn_prompt(q, k, v):
    nb, _, t, _ = k.shape
    tile = min(ATT_TILE, t)
    assert tile % (DILATIONS[-1] * SUBLANES) == 0 and t % tile == 0 and t >= DILATIONS[-1] * KEY_ROWS
    whole = pl.BlockSpec((None, N_PAIRS, t, LANES), lambda b, i, g: (b, 0, 0, 0),
                         pipeline_mode=pl.Buffered(1))
    scratch = pltpu.VMEM((N_PAIRS, tile, LANES), F32)
    return pl.pallas_call(
        functools.partial(_attn_prompt_kernel, tile=tile),
        grid=(nb, t // tile, N_GROUPS),
        in_specs=[pl.BlockSpec((None, N_PAIRS, tile, LANES), lambda b, i, g: (b, g, i, 0)), whole, whole],
        out_specs=pl.BlockSpec((None, N_PAIRS, tile, LANES), lambda b, i, g: (b, 0, i, 0)),
        out_shape=jax.ShapeDtypeStruct(k.shape, F32),
        scratch_shapes=[scratch] * 3,
        compiler_params=_params("parallel", "arbitrary", "arbitrary"),
        name="attn_prompt",
    )(q, k, v)


def _attn_sample_kernel(q_ref, kn_ref, vn_ref, kt_ref, vt_ref, o_ref, kn_s, vn_s):
    steps, past = q_ref.shape[0], kt_ref.shape[-1]
    nrow = steps * N_KV_HEADS
    kn_s[...] = jnp.zeros_like(kn_s)
    vn_s[...] = jnp.zeros_like(vn_s)
    kn_s[0:steps, :] = kn_ref[...]
    vn_s[0:steps, :] = vn_ref[...]
    kn = kn_s[...].astype(BF16)
    vn = vn_s[...].astype(BF16)
    kt = kt_ref[...].reshape(KV_DIM, past).astype(BF16)
    vt = vt_ref[...].reshape(KV_DIM, past).astype(BF16)

    row = lax.broadcasted_iota(jnp.int32, (nrow, KV_DIM), 0)
    lane = lax.broadcasted_iota(jnp.int32, (nrow, KV_DIM), 1)
    head_lanes = (lane >> 6) == (row & (N_KV_HEADS - 1))
    q = q_ref[...] * SCALE

    def q_rows(g):
        qg = q[:, g * KV_DIM:(g + 1) * KV_DIM]
        rep = jnp.concatenate(
            [jnp.broadcast_to(qg[s:s + 1, :], (N_KV_HEADS, KV_DIM)) for s in range(steps)], axis=0)
        return jnp.where(head_lanes, rep, 0.0)

    qbd = jnp.concatenate([q_rows(g) for g in range(N_GROUPS)], axis=0).astype(BF16)
    s_win = _dot(qbd, kt)
    s_new = lax.dot_general(qbd, kn, _NT, preferred_element_type=F32)

    def masked(scores, first_pos):
        width = scores.shape[1]
        srow = lax.broadcasted_iota(jnp.int32, (nrow, width), 0) >> 3
        col = lax.broadcasted_iota(jnp.int32, (nrow, width), 1)
        dist = (past - first_pos) + srow - col
        in_cache = col < (past + steps - first_pos)
        out = []
        for g, dil in enumerate(DILATIONS):
            ok = (dist >= 0) & (dist <= SPAN * dil) & ((dist & (dil - 1)) == 0) & in_cache
            out.append(jnp.where(ok, scores[g * nrow:(g + 1) * nrow], -jnp.inf))
        return out

    parts = masked(s_win, 0) + masked(s_new, past)
    top = functools.reduce(jnp.maximum, [jnp.max(s, axis=-1, keepdims=True) for s in parts])
    probs = [jnp.exp(s - top) for s in parts]
    den = functools.reduce(jnp.add, [jnp.sum(p, axis=-1, keepdims=True) for p in probs])
    p_win = jnp.concatenate(probs[:N_GROUPS], axis=0).astype(BF16)
    p_new = jnp.concatenate(probs[N_GROUPS:], axis=0).astype(BF16)
    acc = lax.dot_general(p_win, vt, _NT, preferred_element_type=F32) + _dot(p_new, vn)
    acc = functools.reduce(jnp.add, [acc[g * nrow:(g + 1) * nrow] for g in range(N_GROUPS)])
    acc = jnp.where(head_lanes, acc / den, 0.0)
    o_ref[...] = jnp.sum(acc.reshape(steps, N_KV_HEADS, KV_DIM), axis=1)


def _attn_sample(q, kn, vn, cache_k, cache_v):
    nb, steps, _ = kn.shape
    past, heads, hd = cache_k.shape[1:]
    assert past % LANES == 0 and steps <= SUBLANES, (past, steps)
    views = [jnp.transpose(c, (0, 2, 3, 1)) for c in (cache_k, cache_v)]
    cache_spec = pl.BlockSpec((None, heads, hd, past), lambda b: (b, 0, 0, 0))
    new_spec = pl.BlockSpec((None, steps, KV_DIM), lambda b: (b, 0, 0))
    return pl.pallas_call(
        _attn_sample_kernel,
        grid=(nb,),
        in_specs=[pl.BlockSpec((None, steps, N_GROUPS * KV_DIM), lambda b: (b, 0, 0)),
                  new_spec, new_spec, cache_spec, cache_spec],
        out_specs=new_spec,
        out_shape=jax.ShapeDtypeStruct(kn.shape, F32),
        scratch_shapes=[pltpu.VMEM((SPAN, KV_DIM), F32), pltpu.VMEM((SPAN, KV_DIM), F32)],
        compiler_params=_params("parallel"),
        name="attn_sample",
    )(q, kn, vn, *views)


def _rope_tables(pos):
    half = ROT_DIM // 2
    inv = ROPE_THETA ** (-jnp.arange(half, dtype=F32) * (2.0 / ROT_DIM))
    dlane = np.arange(LANES) % HEAD_DIM
    inv_lane = jnp.where(dlane < ROT_DIM, inv[dlane % half], 0.0)
    ang = pos.astype(F32)[:, None] * inv_lane[None, :]
    sign = np.where(dlane < half, -1.0, 1.0).astype(np.float32)
    return jnp.cos(ang), jnp.sin(ang) * sign


def _trunk(x, pos, conv_a_prev, h0, ffn_prev, w, attend, *, tstride):
    n_a = w["w_in"].shape[0]
    depth = w["w_up"].shape[0]
    cos, sin = _rope_tables(pos)
    hs_new, conv_new, ffn_new = [], [], []
    k = v = None
    for layer in range(depth):
        att = wo = None
        if layer < n_a:
            x, cs, hl = _mixer_a(x, conv_a_prev[layer], h0[layer], w["norm_a"][layer], w["w_in"][layer],
                                 w["conv_a_w"][layer], w["conv_a_b"][layer], w["gate_w"][layer],
                                 w["gate_b"][layer], w["lam"][layer], w["w_out"][layer], tstride=tstride)
            conv_new.append(cs)
            hs_new.append(hl)
        else:
            j = layer - n_a
            (q,) = _project(x, w["norm_b"][j], w["w_q"][j], w["hsum"], w["q_norm"][j], cos, sin,
                            n_rope=w["w_q"].shape[-1])
            att, wo = attend(q, k, v), w["w_o"][j]
        x, fs = _conv_ffn(x, att, wo, ffn_prev[layer], w["norm_ffn"][layer], w["w_up"][layer],
                          w["ffn_conv_w"][layer], w["ffn_conv_b"][layer], w["w_down"][layer],
                          tstride=tstride)
        ffn_new.append(fs)
        if layer == n_a - 1:
            k, v = _project(x, w["norm_kv"], w["w_kv"], w["hsum"], w["k_norm"], cos, sin, n_rope=KV_DIM)
    return x, k, v, jnp.stack(hs_new), jnp.stack(conv_new), jnp.stack(ffn_new)


def kernel(x_prompt, x_sample, cache_k, cache_v, state_rglru_h, state_rglru_conv, state_ffn_conv, norm_mix_a, w_in_a, conv_a_w, conv_a_b, gate_r_w, gate_r_b, gate_i_w, gate_i_b, lru_lambda, w_out_a, norm_kv, w_kv, k_norm, norm_mix_b, w_q, q_norm, w_o, norm_ffn, w_ffn_up, ffn_conv_w, ffn_conv_b, w_ffn_down):
    n_a, d = norm_mix_a.shape
    depth, _, dff2 = w_ffn_up.shape
    dff = dff2 // 2
    bp, tp, _ = x_prompt.shape
    bs, ts, _ = x_sample.shape

    head_of = np.arange(LANES) // HEAD_DIM
    w = dict(
        norm_a=norm_mix_a[:, None, :], w_in=w_in_a.astype(BF16),
        conv_a_w=conv_a_w, conv_a_b=conv_a_b[:, None, :],
        gate_w=jnp.concatenate([gate_r_w, gate_i_w], axis=-1).astype(BF16),
        gate_b=jnp.concatenate([gate_r_b, gate_i_b], axis=-1)[:, :, None, :],
        lam=lru_lambda[:, None, :], w_out=w_out_a.astype(BF16),
        norm_kv=norm_kv[None, :], w_kv=w_kv.astype(BF16),
        k_norm=jnp.tile(k_norm, LANES // HEAD_DIM)[None, :],
        norm_b=norm_mix_b[:, None, :], w_q=w_q.astype(BF16),
        q_norm=jnp.tile(q_norm, (1, LANES // HEAD_DIM))[:, None, :],
        w_o=w_o.astype(BF16), norm_ffn=norm_ffn[:, None, :],
        w_up=w_ffn_up.astype(BF16), ffn_conv_w=ffn_conv_w, ffn_conv_b=ffn_conv_b[:, None, :],
        w_down=w_ffn_down.astype(BF16),
        hsum=jnp.asarray((head_of[:, None] == head_of[None, :]) / HEAD_DIM, BF16),
    )

    y_prompt, k_p, v_p, p_h, p_conv, p_ffn = _trunk(
        x_prompt, jnp.arange(tp),
        jnp.zeros((n_a, bp, CONV_A_WIDTH - 1, d), F32), jnp.zeros((n_a, bp, 1, d), F32),
        jnp.zeros((depth, bp, FFN_CONV_WIDTH - 1, dff), F32), w, _attn_prompt, tstride=1)
    keep = min(WINDOWS[-1], tp)

    def last_rows(a):
        return jnp.swapaxes(a[:, :, tp - keep:], 1, 2).reshape(bp, keep, N_KV_HEADS, HEAD_DIM)

    p_cache_k, p_cache_v = last_rows(k_p), last_rows(v_p)

    def to_tm(a):
        a = jnp.swapaxes(a, -3, -2)
        return a.reshape(a.shape[:-3] + (1, a.shape[-3] * bs, a.shape[-1]))

    def from_tm(a):
        a = a.reshape(a.shape[:-3] + (a.shape[-2] // bs, bs, a.shape[-1]))
        return jnp.swapaxes(a, -3, -2)

    def to_tokens(a):
        a = a.reshape(a.shape[1], ts, bs, LANES)
        return jnp.transpose(a, (2, 1, 0, 3)).reshape(bs, ts, -1)

    def from_tokens(a):
        a = a.reshape(bs, ts, -1, LANES)
        return jnp.transpose(a, (2, 1, 0, 3)).reshape(1, -1, ts * bs, LANES)

    def attend_sample(q, k, v):
        return from_tokens(_attn_sample(to_tokens(q), to_tokens(k), to_tokens(v), cache_k, cache_v))

    pos_s = PAST_LEN + jnp.repeat(jnp.arange(ts), bs)
    y_s, k_s, v_s, s_h, s_conv, s_ffn = _trunk(
        to_tm(x_sample), pos_s, to_tm(state_rglru_conv), state_rglru_h[:, None], to_tm(state_ffn_conv),
        w, attend_sample, tstride=bs)

    return (y_prompt, from_tm(y_s), p_h[:, :, 0, :], p_conv, p_ffn, p_cache_k, p_cache_v,
            s_h[:, 0], from_tm(s_conv), from_tm(s_ffn),
            to_tokens(k_s).reshape(bs, ts, N_KV_HEADS, HEAD_DIM),
            to_tokens(v_s).reshape(bs, ts, N_KV_HEADS, HEAD_DIM))
```

```python
import functools
import math

import numpy as np
import jax
import jax.numpy as jnp
from jax import lax
from jax.experimental import pallas as pl
from jax.experimental.pallas import tpu as pltpu

F32 = jnp.float32
BF16 = jnp.bfloat16

N_LRU_BLOCKS = 4
LRU_C = 8.0
CONV_A_WIDTH = 4
FFN_CONV_WIDTH = 3
HEAD_DIM = 64
N_KV_HEADS = 8
KV_DIM = N_KV_HEADS * HEAD_DIM
WINDOWS = (128, 512, 2048)
DILATIONS = (1, 4, 16)
N_GROUPS = len(WINDOWS)
SPAN = 128
ROT_DIM = HEAD_DIM // 4
ROPE_THETA = 500000.0
PAST_LEN = 2048
EPS = 1e-6
SCALE = HEAD_DIM ** -0.5

LANES = 128
SUBLANES = 8
N_PAIRS = KV_DIM // LANES
ATT_TILE = 2048
MERGE_ROWS = 128
LOG2E = 1.4426950408889634
LN2 = 0.6931471805599453
KEY_ROWS = 2 * SPAN
ROW_TILE = 512
FF_CHUNK = 3072
PROJ_CHUNK = 512
VMEM_LIMIT = 56 * 1024 * 1024

_NT = (((1,), (1,)), ((), ()))


def _rmsnorm(x, g):
    ms = jnp.mean(x * x, axis=-1, keepdims=True)
    return (x * lax.rsqrt(ms + EPS)) * g


def _dot(a, b):
    return jnp.dot(a, b, preferred_element_type=F32)


def _sigmoid(x):
    return 1.0 / (1.0 + jnp.exp2(x * (-LOG2E)))


def _gelu(x):
    k1 = -2.0 * math.sqrt(2.0 / math.pi) * LOG2E
    return x / (1.0 + jnp.exp2(x * (k1 + (k1 * 0.044715) * (x * x))))


def _const_spec(shape, single_buffer=False):
    nd = len(shape)
    kw = {"pipeline_mode": pl.Buffered(1)} if single_buffer else {}
    return pl.BlockSpec(shape, lambda *_: (0,) * nd, **kw)


def _layer_spec(stacked, layer, single_buffer=False):
    rest = stacked.shape[1:]
    kw = {"pipeline_mode": pl.Buffered(1)} if single_buffer else {}
    return pl.BlockSpec((None,) + rest, lambda *_: (layer,) + (0,) * len(rest), **kw)


def _params(*sem):
    return pltpu.CompilerParams(dimension_semantics=sem, vmem_limit_bytes=VMEM_LIMIT)


def _conv_pad(width, tstride):
    hist = (width - 1) * tstride
    return hist, max(SUBLANES, hist)


def _causal_conv(cat_ref, cols, w_ref, b_ref, width, tstride, rows):
    hist, pad = _conv_pad(width, tstride)
    acc = None
    for j in range(width):
        start = pad - hist + j * tstride
        term = cat_ref[start:start + rows, cols] * w_ref[j:j + 1, cols]
        acc = term if acc is None else acc + term
    return b_ref[:, cols] + acc


def _mixer_a_kernel(x_ref, cprev_ref, h0_ref, g_ref, win_ref, cw_ref, cb_ref, gw_ref, gb_ref,
                    lam_ref, wout_ref, xo_ref, cst_ref, hl_ref, cat_s, hc_s, a_s, u_s,
                    *, tstride, rows):
    d = x_ref.shape[-1]
    blk = d // N_LRU_BLOCKS
    hist, pad = _conv_pad(CONV_A_WIDTH, tstride)

    @pl.when(pl.program_id(1) == 0)
    def _():
        cat_s[pad - hist:pad, :] = cprev_ref[...]
        hc_s[...] = h0_ref[...]

    x = x_ref[...]
    hn = _rmsnorm(x, g_ref[...]).astype(BF16)
    y = _dot(hn, win_ref[...])
    gate = y[:, :d]
    cat_s[pad:pad + rows, :] = y[:, d:]
    xc = _causal_conv(cat_s, slice(None), cw_ref, cb_ref, CONV_A_WIDTH, tstride, rows)
    new_hist = cat_s[pad + rows - hist:pad + rows, :]
    cat_s[pad - hist:pad, :] = new_hist
    cst_ref[...] = new_hist

    lam = lam_ref[...]
    nsp = -lam
    softplus = jnp.maximum(nsp, 0.0) + jnp.log1p(jnp.exp(-jnp.abs(nsp)))
    decay = (-LRU_C * LOG2E) * softplus
    xcb = xc.astype(BF16)
    for n in range(N_LRU_BLOCKS):
        cols = slice(n * blk, (n + 1) * blk)
        ri = _dot(xcb[:, cols], gw_ref[n]) + gb_ref[n]
        r = _sigmoid(ri[:, :blk])
        i = _sigmoid(ri[:, blk:])
        a = jnp.exp2(r * decay[:, cols])
        a_s[:, cols] = a
        gain2 = 1.0 - a * a
        gain = jnp.where(gain2 > 0.0, gain2 * lax.rsqrt(gain2), 0.0)
        u_s[:, cols] = gain * (i * xc[:, cols])

    if tstride == 1:
        groups = rows // SUBLANES
        a3 = a_s[...].reshape(groups, SUBLANES, d)
        u3 = u_s[...].reshape(groups, SUBLANES, d)
        sub = lax.broadcasted_iota(jnp.int32, a3.shape, 1)
        for k in (1, 2, 4):
            keep = sub >= k
            u3 = jnp.where(keep, a3 * pltpu.roll(u3, k, 1) + u3, u3)
            a3 = jnp.where(keep, a3 * pltpu.roll(a3, k, 1), a3)
        a_s[...] = a3.reshape(rows, d)
        u_s[...] = u3.reshape(rows, d)

        def carry(gi, hb):
            r0 = pl.multiple_of(gi * SUBLANES, SUBLANES)
            h = a_s[pl.ds(r0, SUBLANES), :] * hb + u_s[pl.ds(r0, SUBLANES), :]
            u_s[pl.ds(r0, SUBLANES), :] = h
            return jnp.broadcast_to(h[SUBLANES - 1:SUBLANES, :], (SUBLANES, d))

        hb = lax.fori_loop(0, groups, carry, jnp.broadcast_to(hc_s[...], (SUBLANES, d)))
        h_last = hb[0:1, :]
    else:
        h_last = hc_s[...]
        for t in range(rows // tstride):
            sl = slice(t * tstride, (t + 1) * tstride)
            h_last = a_s[sl, :] * h_last + u_s[sl, :]
            u_s[sl, :] = h_last
    hc_s[...] = h_last
    hl_ref[...] = h_last

    yv = (u_s[...] * _gelu(gate)).astype(BF16)
    xo_ref[...] = x + _dot(yv, wout_ref[...])


def _mixer_a(x, cprev, h0, g, win, cw, cb, gw, gb, lam, wout, *, layer, tstride):
    nb, t, d = x.shape
    ls = functools.partial(_layer_spec, layer=layer)
    rows = min(ROW_TILE, t)
    hist, pad = _conv_pad(CONV_A_WIDTH, tstride)
    row_spec = pl.BlockSpec((None, rows, d), lambda b, i: (b, i, 0))
    per_b = lambda n: pl.BlockSpec((None, n, d), lambda b, i: (b, 0, 0))
    return pl.pallas_call(
        functools.partial(_mixer_a_kernel, tstride=tstride, rows=rows),
        grid=(nb, t // rows),
        in_specs=[row_spec, per_b(hist), per_b(tstride), ls(g), ls(win, single_buffer=True), ls(cw),
                  ls(cb), ls(gw, single_buffer=True), ls(gb), ls(lam), ls(wout, single_buffer=True)],
        out_specs=[row_spec, per_b(hist), per_b(tstride)],
        out_shape=[jax.ShapeDtypeStruct(x.shape, F32),
                   jax.ShapeDtypeStruct((nb, hist, d), F32),
                   jax.ShapeDtypeStruct((nb, tstride, d), F32)],
        scratch_shapes=[pltpu.VMEM((pad + rows, d), F32), pltpu.VMEM((tstride, d), F32),
                        pltpu.VMEM((rows, d), F32), pltpu.VMEM((rows, d), F32)],
        compiler_params=_params("parallel", "arbitrary"),
        name="mixer_a",
    )(x, cprev, h0, g, win, cw, cb, gw, gb, lam, wout)


def _ffn_kernel(*refs, tstride, rows, has_att):
    x_ref = refs[0]
    if has_att:
        att_ref, wo_ref = refs[1:3]
    prev_ref, g_ref, wup_ref, cw_ref, cb_ref, wdn_ref, xo_ref, st_ref, cat_s = refs[1 + 2 * has_att:]
    dff = wdn_ref.shape[0]
    hist, pad = _conv_pad(FFN_CONV_WIDTH, tstride)

    @pl.when(pl.program_id(1) == 0)
    def _():
        cat_s[pad - hist:pad, :] = prev_ref[...]

    x = x_ref[...]
    if has_att:
        merged = jnp.concatenate([att_ref[p] for p in range(att_ref.shape[0])], axis=1)
        x = x + _dot(merged.astype(BF16), wo_ref[...])

    hn = _rmsnorm(x, g_ref[...]).astype(BF16)
    acc = None
    for c in range(dff // FF_CHUNK):
        cols = slice(c * FF_CHUNK, (c + 1) * FF_CHUNK)
        vcols = slice(dff + c * FF_CHUNK, dff + (c + 1) * FF_CHUNK)
        cat_s[pad:pad + rows, cols] = _dot(hn, wup_ref[:, cols])
        gc = _causal_conv(cat_s, cols, cw_ref, cb_ref, FFN_CONV_WIDTH, tstride, rows)
        act = (_gelu(gc) * _dot(hn, wup_ref[:, vcols])).astype(BF16)
        part = _dot(act, wdn_ref[cols, :])
        acc = part if acc is None else acc + part
    new_hist = cat_s[pad + rows - hist:pad + rows, :]
    cat_s[pad - hist:pad, :] = new_hist
    st_ref[...] = new_hist
    xo_ref[...] = x + acc


def _conv_ffn(x, att, wo, prev, g, wup, cw, cb, wdn, *, layer, att_layer, tstride):
    nb, t, d = x.shape
    dff = wdn.shape[1]
    ls = functools.partial(_layer_spec, layer=layer)
    rows = min(ROW_TILE, t)
    hist, pad = _conv_pad(FFN_CONV_WIDTH, tstride)
    row_spec = pl.BlockSpec((None, rows, d), lambda b, i: (b, i, 0))
    st_spec = pl.BlockSpec((None, hist, dff), lambda b, i: (b, 0, 0))
    has_att = att is not None
    att_args, att_specs = (), []
    if has_att:
        att_args = (att, wo)
        att_specs = [pl.BlockSpec((None, N_PAIRS, rows, LANES), lambda b, i: (b, 0, i, 0)),
                     _layer_spec(wo, att_layer)]
    return pl.pallas_call(
        functools.partial(_ffn_kernel, tstride=tstride, rows=rows, has_att=has_att),
        grid=(nb, t // rows),
        in_specs=[row_spec] + att_specs
        + [st_spec, ls(g), ls(wup, single_buffer=True), ls(cw), ls(cb), ls(wdn, single_buffer=True)],
        out_specs=[row_spec, st_spec],
        out_shape=[jax.ShapeDtypeStruct(x.shape, F32), jax.ShapeDtypeStruct((nb, hist, dff), F32)],
        scratch_shapes=[pltpu.VMEM((pad + rows, dff), F32)],
        compiler_params=_params("parallel", "arbitrary"),
        name="conv_ffn",
    )(x, *att_args, prev, g, wup, cw, cb, wdn)


def _proj_kernel(x_ref, g_ref, w_ref, hsum_ref, gh_ref, cos_ref, sin_ref, *out_refs, n_rope):
    hn = _rmsnorm(x_ref[...], g_ref[...]).astype(BF16)
    rows = hn.shape[0]
    lane = lax.broadcasted_iota(jnp.int32, (rows, LANES), 1)
    first_half = (lane & (HEAD_DIM - 1)) < (ROT_DIM // 2)
    hsum = hsum_ref[...]
    cos = cos_ref[...]
    sin = sin_ref[...]
    for c in range(w_ref.shape[1] // PROJ_CHUNK):
        y = _dot(hn, w_ref[:, c * PROJ_CHUNK:(c + 1) * PROJ_CHUNK])
        for j in range(PROJ_CHUNK // LANES):
            i = c * (PROJ_CHUNK // LANES) + j
            yt = y[:, j * LANES:(j + 1) * LANES]
            if i >= n_rope // LANES:
                out_refs[1][i - n_rope // LANES] = yt
                continue
            sq = yt * yt
            hi = sq.astype(BF16)
            lo = (sq - hi.astype(F32)).astype(BF16)
            ms = _dot(hi, hsum) + _dot(lo, hsum)
            yn = (yt * lax.rsqrt(ms + EPS)) * gh_ref[...]
            partner = jnp.where(first_half, pltpu.roll(yn, LANES - ROT_DIM // 2, 1),
                                pltpu.roll(yn, ROT_DIM // 2, 1))
            out_refs[0][i] = yn * cos + partner * sin


def _project(x, g, w, hsum, gh, cos, sin, *, layer, n_rope):
    nb, t, d = x.shape
    n = w.shape[2]
    rows = min(ROW_TILE, t)
    row_in = pl.BlockSpec((None, rows, d), lambda b, i: (b, i, 0))
    tab = pl.BlockSpec((rows, LANES), lambda b, i: (i, 0))
    tiles = [n_rope // LANES] + ([(n - n_rope) // LANES] if n > n_rope else [])
    return pl.pallas_call(
        functools.partial(_proj_kernel, n_rope=n_rope),
        grid=(nb, t // rows),
        in_specs=[row_in, _layer_spec(g, layer), _layer_spec(w, layer), _const_spec(hsum.shape),
                  _layer_spec(gh, layer), tab, tab],
        out_specs=[pl.BlockSpec((None, nt, rows, LANES), lambda b, i: (b, 0, i, 0)) for nt in tiles],
        out_shape=[jax.ShapeDtypeStruct((nb, nt, t, LANES), F32) for nt in tiles],
        compiler_params=_params("parallel", "parallel"),
        name="norm_proj_rope",
    )(x, g, w, hsum, gh, cos, sin)


def _attn_group(q_ref, k_ref, v_ref, og_s, lg_s, tile_start, dil, tile):
    per_res = tile // dil
    nq = min(SPAN, per_res)
    nsub = per_res // nq
    sub0 = tile_start // dil
    qi = lax.broadcasted_iota(jnp.int32, (2 * nq, KEY_ROWS), 0) & (nq - 1)
    kj = lax.broadcasted_iota(jnp.int32, (2 * nq, KEY_ROWS), 1)
    rel = qi - kj
    low = lax.broadcasted_iota(jnp.int32, (nq, LANES), 1) < HEAD_DIM

    def rows(start, n):
        return pl.ds(start, n, stride=dil) if dil > 1 else pl.ds(start, n)

    def block(idx, carry):
        r = idx // nsub
        j = idx - r * nsub
        q0 = sub0 + j * nq
        k0 = jnp.maximum(q0 + nq - KEY_ROWS, 0)
        dist = rel + (q0 - k0)
        bias = jnp.where((dist >= 0) & (dist <= SPAN), 0.0, -jnp.inf)
        q_rows = rows(r + dil * (j * nq), nq)
        k_rows = rows(r + dil * k0, KEY_ROWS)
        for p in range(N_PAIRS):
            qp = (q_ref[p, q_rows, :] * (SCALE * LOG2E)).astype(BF16)
            zero = jnp.zeros_like(qp)
            qs = jnp.concatenate([jnp.where(low, qp, zero), jnp.where(low, zero, qp)], axis=0)
            s = lax.dot_general(qs, k_ref[p, k_rows, :].astype(BF16), _NT, preferred_element_type=F32)
            s = s + bias
            m = jnp.max(s, axis=-1, keepdims=True)
            pe = jnp.exp2(s - m)
            den = jnp.sum(pe, axis=-1, keepdims=True)
            o = _dot(pe.astype(BF16), v_ref[p, k_rows, :].astype(BF16)) / den
            lse = jnp.broadcast_to(m * LN2 + jnp.log(den), o.shape)
            og_s[p, q_rows, :] = jnp.where(low, o[:nq], o[nq:])
            lg_s[p, q_rows, :] = jnp.where(low, lse[:nq], lse[nq:])
        return carry

    lax.fori_loop(0, dil * nsub, block, 0, unroll=2)


def _attn_prompt_kernel(q_ref, k_ref, v_ref, o_ref, og_s, lg_s, la_s, *, tile):
    group = pl.program_id(2)
    tile_start = pl.program_id(1) * tile
    for gi, dil in enumerate(DILATIONS):
        @pl.when(group == gi)
        def _(gi=gi, dil=dil):
            dst = (o_ref, la_s) if gi == 0 else (og_s, lg_s)
            _attn_group(q_ref, k_ref, v_ref, *dst, tile_start, dil, tile)

    @pl.when(group > 0)
    def _():
        def merge(c, carry):
            rows = pl.ds(pl.multiple_of(c * MERGE_ROWS, MERGE_ROWS), MERGE_ROWS)
            for p in range(N_PAIRS):
                la, lg = la_s[p, rows, :], lg_s[p, rows, :]
                top = jnp.maximum(la, lg)
                ea, eg = jnp.exp(la - top), jnp.exp(lg - top)
                tot = ea + eg
                o_ref[p, rows, :] = (ea * o_ref[p, rows, :] + eg * og_s[p, rows, :]) / tot
                la_s[p, rows, :] = top + jnp.log(tot)
            return carry

        lax.fori_loop(0, tile // MERGE_ROWS, merge, 0)


def _attn_prompt(q, k, v):
    nb, _, t, _ = k.shape
    tile = min(ATT_TILE, t)
    assert tile % (DILATIONS[-1] * SUBLANES) == 0 and t % tile == 0 and t >= DILATIONS[-1] * KEY_ROWS
    whole = pl.BlockSpec((None, N_PAIRS, t, LANES), lambda b, i, g: (b, 0, 0, 0),
                         pipeline_mode=pl.Buffered(1))
    scratch = pltpu.VMEM((N_PAIRS, tile, LANES), F32)
    return pl.pallas_call(
        functools.partial(_attn_prompt_kernel, tile=tile),
        grid=(nb, t // tile, N_GROUPS),
        in_specs=[pl.BlockSpec((None, N_PAIRS, tile, LANES), lambda b, i, g: (b, g, i, 0)), whole, whole],
        out_specs=pl.BlockSpec((None, N_PAIRS, tile, LANES), lambda b, i, g: (b, 0, i, 0)),
        out_shape=jax.ShapeDtypeStruct(k.shape, F32),
        scratch_shapes=[scratch] * 3,
        compiler_params=_params("parallel", "arbitrary", "arbitrary"),
        name="attn_prompt",
    )(q, k, v)


def _attn_sample_kernel(q_ref, kn_ref, vn_ref, kt_ref, vt_ref, o_ref, kn_s, vn_s):
    steps, past = q_ref.shape[0], kt_ref.shape[-1]
    nrow = steps * N_KV_HEADS
    kn_s[...] = jnp.zeros_like(kn_s)
    vn_s[...] = jnp.zeros_like(vn_s)
    kn_s[0:steps, :] = kn_ref[...]
    vn_s[0:steps, :] = vn_ref[...]
    kn = kn_s[...].astype(BF16)
    vn = vn_s[...].astype(BF16)
    kt = kt_ref[...].reshape(KV_DIM, past).astype(BF16)
    vt = vt_ref[...].reshape(KV_DIM, past).astype(BF16)

    row = lax.broadcasted_iota(jnp.int32, (nrow, KV_DIM), 0)
    lane = lax.broadcasted_iota(jnp.int32, (nrow, KV_DIM), 1)
    head_lanes = (lane >> 6) == (row & (N_KV_HEADS - 1))
    q = q_ref[...] * SCALE

    def q_rows(g):
        qg = q[:, g * KV_DIM:(g + 1) * KV_DIM]
        rep = jnp.concatenate(
            [jnp.broadcast_to(qg[s:s + 1, :], (N_KV_HEADS, KV_DIM)) for s in range(steps)], axis=0)
        return jnp.where(head_lanes, rep, 0.0)

    qbd = jnp.concatenate([q_rows(g) for g in range(N_GROUPS)], axis=0).astype(BF16)
    s_win = _dot(qbd, kt)
    s_new = lax.dot_general(qbd, kn, _NT, preferred_element_type=F32)

    def masked(scores, first_pos):
        width = scores.shape[1]
        srow = lax.broadcasted_iota(jnp.int32, (nrow, width), 0) >> 3
        col = lax.broadcasted_iota(jnp.int32, (nrow, width), 1)
        dist = (past - first_pos) + srow - col
        in_cache = col < (past + steps - first_pos)
        out = []
        for g, dil in enumerate(DILATIONS):
            ok = (dist >= 0) & (dist <= SPAN * dil) & ((dist & (dil - 1)) == 0) & in_cache
            out.append(jnp.where(ok, scores[g * nrow:(g + 1) * nrow], -jnp.inf))
        return out

    parts = masked(s_win, 0) + masked(s_new, past)
    top = functools.reduce(jnp.maximum, [jnp.max(s, axis=-1, keepdims=True) for s in parts])
    probs = [jnp.exp(s - top) for s in parts]
    den = functools.reduce(jnp.add, [jnp.sum(p, axis=-1, keepdims=True) for p in probs])
    p_win = jnp.concatenate(probs[:N_GROUPS], axis=0).astype(BF16)
    p_new = jnp.concatenate(probs[N_GROUPS:], axis=0).astype(BF16)
    acc = lax.dot_general(p_win, vt, _NT, preferred_element_type=F32) + _dot(p_new, vn)
    acc = functools.reduce(jnp.add, [acc[g * nrow:(g + 1) * nrow] for g in range(N_GROUPS)])
    acc = jnp.where(head_lanes, acc / den, 0.0)
    o_ref[...] = jnp.sum(acc.reshape(steps, N_KV_HEADS, KV_DIM), axis=1)


def _attn_sample(q, kn, vn, cache_k, cache_v):
    nb, steps, _ = kn.shape
    past, heads, hd = cache_k.shape[1:]
    assert past % LANES == 0 and steps <= SUBLANES, (past, steps)
    views = [jnp.transpose(c, (0, 2, 3, 1)) for c in (cache_k, cache_v)]
    cache_spec = pl.BlockSpec((None, heads, hd, past), lambda b: (b, 0, 0, 0))
    new_spec = pl.BlockSpec((None, steps, KV_DIM), lambda b: (b, 0, 0))
    return pl.pallas_call(
        _attn_sample_kernel,
        grid=(nb,),
        in_specs=[pl.BlockSpec((None, steps, N_GROUPS * KV_DIM), lambda b: (b, 0, 0)),
                  new_spec, new_spec, cache_spec, cache_spec],
        out_specs=new_spec,
        out_shape=jax.ShapeDtypeStruct(kn.shape, F32),
        scratch_shapes=[pltpu.VMEM((SPAN, KV_DIM), F32), pltpu.VMEM((SPAN, KV_DIM), F32)],
        compiler_params=_params("parallel"),
        name="attn_sample",
    )(q, kn, vn, *views)


def _rope_tables(pos):
    half = ROT_DIM // 2
    inv = ROPE_THETA ** (-jnp.arange(half, dtype=F32) * (2.0 / ROT_DIM))
    dlane = np.arange(LANES) % HEAD_DIM
    inv_lane = jnp.where(dlane < ROT_DIM, inv[dlane % half], 0.0)
    ang = pos.astype(F32)[:, None] * inv_lane[None, :]
    sign = np.where(dlane < half, -1.0, 1.0).astype(np.float32)
    return jnp.cos(ang), jnp.sin(ang) * sign


def _trunk(x, pos, conv_a_prev, h0, ffn_prev, w, attend, *, tstride):
    n_a = w["w_in"].shape[0]
    depth = w["w_up"].shape[0]
    cos, sin = _rope_tables(pos)
    hs_new, conv_new, ffn_new = [], [], []
    k = v = None
    for layer in range(depth):
        att, j = None, max(layer - n_a, 0)
        if layer < n_a:
            x, cs, hl = _mixer_a(x, conv_a_prev[layer], h0[layer], w["norm_a"], w["w_in"], w["conv_a_w"],
                                 w["conv_a_b"], w["gate_w"], w["gate_b"], w["lam"], w["w_out"],
                                 layer=layer, tstride=tstride)
            conv_new.append(cs)
            hs_new.append(hl)
        else:
            (q,) = _project(x, w["norm_b"], w["w_q"], w["hsum"], w["q_norm"], cos, sin,
                            layer=j, n_rope=w["w_q"].shape[-1])
            att = attend(q, k, v)
        x, fs = _conv_ffn(x, att, w["w_o"], ffn_prev[layer], w["norm_ffn"], w["w_up"], w["ffn_conv_w"],
                          w["ffn_conv_b"], w["w_down"], layer=layer, att_layer=j, tstride=tstride)
        ffn_new.append(fs)
        if layer == n_a - 1:
            k, v = _project(x, w["norm_kv"], w["w_kv"], w["hsum"], w["k_norm"], cos, sin,
                            layer=0, n_rope=KV_DIM)
    return x, k, v, jnp.stack(hs_new), jnp.stack(conv_new), jnp.stack(ffn_new)


def kernel(x_prompt, x_sample, cache_k, cache_v, state_rglru_h, state_rglru_conv, state_ffn_conv, norm_mix_a, w_in_a, conv_a_w, conv_a_b, gate_r_w, gate_r_b, gate_i_w, gate_i_b, lru_lambda, w_out_a, norm_kv, w_kv, k_norm, norm_mix_b, w_q, q_norm, w_o, norm_ffn, w_ffn_up, ffn_conv_w, ffn_conv_b, w_ffn_down):
    n_a, d = norm_mix_a.shape
    depth, _, dff2 = w_ffn_up.shape
    dff = dff2 // 2
    bp, tp, _ = x_prompt.shape
    bs, ts, _ = x_sample.shape

    head_of = np.arange(LANES) // HEAD_DIM
    w = dict(
        norm_a=norm_mix_a[:, None, :], w_in=w_in_a.astype(BF16),
        conv_a_w=conv_a_w, conv_a_b=conv_a_b[:, None, :],
        gate_w=jnp.concatenate([gate_r_w, gate_i_w], axis=-1).astype(BF16),
        gate_b=jnp.concatenate([gate_r_b, gate_i_b], axis=-1)[:, :, None, :],
        lam=lru_lambda[:, None, :], w_out=w_out_a.astype(BF16),
        norm_kv=norm_kv[None, None, :], w_kv=w_kv.astype(BF16)[None],
        k_norm=jnp.tile(k_norm, LANES // HEAD_DIM)[None, None, :],
        norm_b=norm_mix_b[:, None, :], w_q=w_q.astype(BF16),
        q_norm=jnp.tile(q_norm, (1, LANES // HEAD_DIM))[:, None, :],
        w_o=w_o.astype(BF16), norm_ffn=norm_ffn[:, None, :],
        w_up=w_ffn_up.astype(BF16), ffn_conv_w=ffn_conv_w, ffn_conv_b=ffn_conv_b[:, None, :],
        w_down=w_ffn_down.astype(BF16),
        hsum=jnp.asarray((head_of[:, None] == head_of[None, :]) / HEAD_DIM, BF16),
    )

    y_prompt, k_p, v_p, p_h, p_conv, p_ffn = _trunk(
        x_prompt, jnp.arange(tp),
        jnp.zeros((n_a, bp, CONV_A_WIDTH - 1, d), F32), jnp.zeros((n_a, bp, 1, d), F32),
        jnp.zeros((depth, bp, FFN_CONV_WIDTH - 1, dff), F32), w, _attn_prompt, tstride=1)
    keep = min(WINDOWS[-1], tp)

    def last_rows(a):
        return jnp.swapaxes(a[:, :, tp - keep:], 1, 2).reshape(bp, keep, N_KV_HEADS, HEAD_DIM)

    p_cache_k, p_cache_v = last_rows(k_p), last_rows(v_p)

    def to_tm(a):
        a = jnp.swapaxes(a, -3, -2)
        return a.reshape(a.shape[:-3] + (1, a.shape[-3] * bs, a.shape[-1]))

    def from_tm(a):
        a = a.reshape(a.shape[:-3] + (a.shape[-2] // bs, bs, a.shape[-1]))
        return jnp.swapaxes(a, -3, -2)

    def to_tokens(a):
        a = a.reshape(a.shape[1], ts, bs, LANES)
        return jnp.transpose(a, (2, 1, 0, 3)).reshape(bs, ts, -1)

    def from_tokens(a):
        a = a.reshape(bs, ts, -1, LANES)
        return jnp.transpose(a, (2, 1, 0, 3)).reshape(1, -1, ts * bs, LANES)

    def attend_sample(q, k, v):
        return from_tokens(_attn_sample(to_tokens(q), to_tokens(k), to_tokens(v), cache_k, cache_v))

    pos_s = PAST_LEN + jnp.repeat(jnp.arange(ts), bs)
    y_s, k_s, v_s, s_h, s_conv, s_ffn = _trunk(
        to_tm(x_sample), pos_s, to_tm(state_rglru_conv), state_rglru_h[:, None], to_tm(state_ffn_conv),
        w, attend_sample, tstride=bs)

    return (y_prompt, from_tm(y_s), p_h[:, :, 0, :], p_conv, p_ffn, p_cache_k, p_cache_v,
            s_h[:, 0], from_tm(s_conv), from_tm(s_ffn),
            to_tokens(k_s).reshape(bs, ts, N_KV_HEADS, HEAD_DIM),
            to_tokens(v_s).reshape(bs, ts, N_KV_HEADS, HEAD_DIM))
```

```python
import functools
import math

import numpy as np
import jax
import jax.numpy as jnp
from jax import lax
from jax.experimental import pallas as pl
from jax.experimental.pallas import tpu as pltpu

F32 = jnp.float32
BF16 = jnp.bfloat16

N_LRU_BLOCKS = 4
LRU_C = 8.0
CONV_A_WIDTH = 4
FFN_CONV_WIDTH = 3
HEAD_DIM = 64
N_KV_HEADS = 8
KV_DIM = N_KV_HEADS * HEAD_DIM
WINDOWS = (128, 512, 2048)
DILATIONS = (1, 4, 16)
N_GROUPS = len(WINDOWS)
SPAN = 128
ROT_DIM = HEAD_DIM // 4
ROPE_THETA = 500000.0
PAST_LEN = 2048
EPS = 1e-6
SCALE = HEAD_DIM ** -0.5

LANES = 128
SUBLANES = 8
N_PAIRS = KV_DIM // LANES
ATT_TILE = 2048
MERGE_ROWS = 128
LOG2E = 1.4426950408889634
LN2 = 0.6931471805599453
KEY_ROWS = 2 * SPAN
ROW_TILE = 512
FF_CHUNK = 3072
PROJ_CHUNK = 512
VMEM_LIMIT = 56 * 1024 * 1024

_NT = (((1,), (1,)), ((), ()))


def _rmsnorm(x, g):
    ms = jnp.mean(x * x, axis=-1, keepdims=True)
    return (x * lax.rsqrt(ms + EPS)) * g


def _dot(a, b):
    return jnp.dot(a, b, preferred_element_type=F32)


def _sigmoid(x):
    return 1.0 / (1.0 + jnp.exp2(x * (-LOG2E)))


def _gelu(x):
    k1 = -2.0 * math.sqrt(2.0 / math.pi) * LOG2E
    return x / (1.0 + jnp.exp2(x * (k1 + (k1 * 0.044715) * (x * x))))


def _const_spec(shape, single_buffer=False):
    nd = len(shape)
    kw = {"pipeline_mode": pl.Buffered(1)} if single_buffer else {}
    return pl.BlockSpec(shape, lambda *_: (0,) * nd, **kw)


def _layer_spec(stacked, layer, single_buffer=False):
    rest = stacked.shape[1:]
    kw = {"pipeline_mode": pl.Buffered(1)} if single_buffer else {}
    return pl.BlockSpec((None,) + rest, lambda *_: (layer,) + (0,) * len(rest), **kw)


def _params(*sem):
    return pltpu.CompilerParams(dimension_semantics=sem, vmem_limit_bytes=VMEM_LIMIT)


def _conv_pad(width, tstride):
    hist = (width - 1) * tstride
    return hist, max(SUBLANES, hist)


def _causal_conv(cat_ref, cols, w_ref, b_ref, width, tstride, rows):
    hist, pad = _conv_pad(width, tstride)
    acc = None
    for j in range(width):
        start = pad - hist + j * tstride
        term = cat_ref[start:start + rows, cols] * w_ref[j:j + 1, cols]
        acc = term if acc is None else acc + term
    return b_ref[:, cols] + acc


def _mixer_a_kernel(*refs, tstride, rows):
    segmented = tstride == 1
    if segmented:
        x_ref, cprev_ref, h0_ref, perm_ref = refs[:4]
        refs = refs[4:]
    else:
        x_ref, cprev_ref, h0_ref = refs[:3]
        refs = refs[3:]
    (g_ref, win_ref, cw_ref, cb_ref, gw_ref, gb_ref, lam_ref, wout_ref,
     xo_ref, cst_ref, hl_ref, cat_s, hc_s, a_s, u_s, ch_s) = refs
    d = x_ref.shape[-1]
    blk = d // N_LRU_BLOCKS
    seg = SUBLANES if segmented else tstride
    steps = rows // seg
    taps = CONV_A_WIDTH - 1
    hist, pad = _conv_pad(CONV_A_WIDTH, seg)

    @pl.when(pl.program_id(1) == 0)
    def _():
        ch_s[...] = cprev_ref[...]
        hc_s[...] = h0_ref[...]

    x = x_ref[...]
    hn = _rmsnorm(x, g_ref[...]).astype(BF16)
    if segmented:
        hn = _dot(perm_ref[0], hn).astype(BF16)
    y = _dot(hn, win_ref[...])
    gate = y[:, :d]
    cat_s[pad:pad + rows, :] = y[:, d:]
    if segmented:
        tail = cat_s[pad + rows - hist:pad + rows, :].reshape(taps, seg, d)
        first = lax.broadcasted_iota(jnp.int32, tail.shape, 1) == 0
        carried = ch_s[...]
        cat_s[pad - hist:pad, :] = jnp.where(first, carried[:, None, :], pltpu.roll(tail, 1, 1)).reshape(hist, d)
        for k in range(taps):
            ch_s[k:k + 1, :] = cat_s[pad + rows - hist + k * seg + seg - 1:pad + rows - hist + (k + 1) * seg, :]
    else:
        cat_s[pad - hist:pad, :] = ch_s[...]
        ch_s[...] = cat_s[pad + rows - hist:pad + rows, :]
    cst_ref[...] = ch_s[...]
    xc = _causal_conv(cat_s, slice(None), cw_ref, cb_ref, CONV_A_WIDTH, seg, rows)

    lam = lam_ref[...]
    nsp = -lam
    softplus = jnp.maximum(nsp, 0.0) + jnp.log1p(jnp.exp(-jnp.abs(nsp)))
    decay = (-LRU_C * LOG2E) * softplus
    xcb = xc.astype(BF16)
    for n in range(N_LRU_BLOCKS):
        cols = slice(n * blk, (n + 1) * blk)
        ri = _dot(xcb[:, cols], gw_ref[n]) + gb_ref[n]
        r = _sigmoid(ri[:, :blk])
        i = _sigmoid(ri[:, blk:])
        a = jnp.exp2(r * decay[:, cols])
        a_s[:, cols] = a
        gain2 = 1.0 - a * a
        gain = jnp.where(gain2 > 0.0, gain2 * lax.rsqrt(gain2), 0.0)
        u_s[:, cols] = gain * (i * xc[:, cols])

    if segmented:
        def step(j, carry):
            h, c = carry
            r0 = pl.multiple_of(j * seg, seg)
            a = a_s[pl.ds(r0, seg), :]
            h = a * h + u_s[pl.ds(r0, seg), :]
            c = a * c
            u_s[pl.ds(r0, seg), :] = h
            a_s[pl.ds(r0, seg), :] = c
            return h, c

        h_end, c_end = lax.fori_loop(0, steps, step, (jnp.zeros((seg, d), F32), jnp.ones((seg, d), F32)),
                                     unroll=4)
        state, starts = hc_s[...], []
        for s in range(seg):
            starts.append(state)
            state = c_end[s:s + 1, :] * state + h_end[s:s + 1, :]
        start = jnp.concatenate(starts, axis=0)
        hs = (u_s[...].reshape(steps, seg, d) + a_s[...].reshape(steps, seg, d) * start[None]).reshape(rows, d)
        h_last = state
    else:
        h_last = hc_s[...]
        for t in range(steps):
            sl = slice(t * seg, (t + 1) * seg)
            h_last = a_s[sl, :] * h_last + u_s[sl, :]
            u_s[sl, :] = h_last
        hs = u_s[...]
    hc_s[...] = h_last
    hl_ref[...] = h_last

    yv = (hs * _gelu(gate)).astype(BF16)
    if segmented:
        yv = _dot(perm_ref[1], yv).astype(BF16)
    xo_ref[...] = x + _dot(yv, wout_ref[...])


def _mixer_a(x, cprev, h0, g, win, cw, cb, gw, gb, lam, wout, *, layer, tstride):
    nb, t, d = x.shape
    ls = functools.partial(_layer_spec, layer=layer)
    rows = min(ROW_TILE, t)
    hist = (CONV_A_WIDTH - 1) * tstride
    seg = SUBLANES if tstride == 1 else tstride
    _, pad = _conv_pad(CONV_A_WIDTH, seg)
    row_spec = pl.BlockSpec((None, rows, d), lambda b, i: (b, i, 0))
    per_b = lambda n: pl.BlockSpec((None, n, d), lambda b, i: (b, 0, 0))
    perm_args, perm_specs = (), []
    if tstride == 1:
        time_of_row = (np.arange(rows) % seg) * (rows // seg) + np.arange(rows) // seg
        perm = np.zeros((rows, rows), np.float32)
        perm[np.arange(rows), time_of_row] = 1.0
        perm_args = (jnp.asarray(np.stack([perm, perm.T]), BF16),)
        perm_specs = [_const_spec((2, rows, rows))]
    return pl.pallas_call(
        functools.partial(_mixer_a_kernel, tstride=tstride, rows=rows),
        grid=(nb, t // rows),
        in_specs=[row_spec, per_b(hist), per_b(tstride)] + perm_specs
        + [ls(g), ls(win, single_buffer=True), ls(cw), ls(cb), ls(gw, single_buffer=True), ls(gb),
           ls(lam), ls(wout, single_buffer=True)],
        out_specs=[row_spec, per_b(hist), per_b(tstride)],
        out_shape=[jax.ShapeDtypeStruct(x.shape, F32),
                   jax.ShapeDtypeStruct((nb, hist, d), F32),
                   jax.ShapeDtypeStruct((nb, tstride, d), F32)],
        scratch_shapes=[pltpu.VMEM((pad + rows, d), F32), pltpu.VMEM((tstride, d), F32),
                        pltpu.VMEM((rows, d), F32), pltpu.VMEM((rows, d), F32),
                        pltpu.VMEM((hist, d), F32)],
        compiler_params=_params("parallel", "arbitrary"),
        name="mixer_a",
    )(x, cprev, h0, *perm_args, g, win, cw, cb, gw, gb, lam, wout)


def _ffn_kernel(*refs, tstride, rows, has_att):
    x_ref = refs[0]
    if has_att:
        att_ref, wo_ref = refs[1:3]
    prev_ref, g_ref, wup_ref, cw_ref, cb_ref, wdn_ref, xo_ref, st_ref, cat_s = refs[1 + 2 * has_att:]
    dff = wdn_ref.shape[0]
    hist, pad = _conv_pad(FFN_CONV_WIDTH, tstride)

    @pl.when(pl.program_id(1) == 0)
    def _():
        cat_s[pad - hist:pad, :] = prev_ref[...]

    x = x_ref[...]
    if has_att:
        merged = jnp.concatenate([att_ref[p] for p in range(att_ref.shape[0])], axis=1)
        x = x + _dot(merged.astype(BF16), wo_ref[...])

    hn = _rmsnorm(x, g_ref[...]).astype(BF16)
    acc = None
    for c in range(dff // FF_CHUNK):
        cols = slice(c * FF_CHUNK, (c + 1) * FF_CHUNK)
        vcols = slice(dff + c * FF_CHUNK, dff + (c + 1) * FF_CHUNK)
        cat_s[pad:pad + rows, cols] = _dot(hn, wup_ref[:, cols])
        gc = _causal_conv(cat_s, cols, cw_ref, cb_ref, FFN_CONV_WIDTH, tstride, rows)
        act = (_gelu(gc) * _dot(hn, wup_ref[:, vcols])).astype(BF16)
        part = _dot(act, wdn_ref[cols, :])
        acc = part if acc is None else acc + part
    new_hist = cat_s[pad + rows - hist:pad + rows, :]
    cat_s[pad - hist:pad, :] = new_hist
    st_ref[...] = new_hist
    xo_ref[...] = x + acc


def _conv_ffn(x, att, wo, prev, g, wup, cw, cb, wdn, *, layer, att_layer, tstride):
    nb, t, d = x.shape
    dff = wdn.shape[1]
    ls = functools.partial(_layer_spec, layer=layer)
    rows = min(ROW_TILE, t)
    hist, pad = _conv_pad(FFN_CONV_WIDTH, tstride)
    row_spec = pl.BlockSpec((None, rows, d), lambda b, i: (b, i, 0))
    st_spec = pl.BlockSpec((None, hist, dff), lambda b, i: (b, 0, 0))
    has_att = att is not None
    att_args, att_specs = (), []
    if has_att:
        att_args = (att, wo)
        att_specs = [pl.BlockSpec((None, N_PAIRS, rows, LANES), lambda b, i: (b, 0, i, 0)),
                     _layer_spec(wo, att_layer)]
    return pl.pallas_call(
        functools.partial(_ffn_kernel, tstride=tstride, rows=rows, has_att=has_att),
        grid=(nb, t // rows),
        in_specs=[row_spec] + att_specs
        + [st_spec, ls(g), ls(wup, single_buffer=True), ls(cw), ls(cb), ls(wdn, single_buffer=True)],
        out_specs=[row_spec, st_spec],
        out_shape=[jax.ShapeDtypeStruct(x.shape, F32), jax.ShapeDtypeStruct((nb, hist, dff), F32)],
        scratch_shapes=[pltpu.VMEM((pad + rows, dff), F32)],
        compiler_params=_params("parallel", "arbitrary"),
        name="conv_ffn",
    )(x, *att_args, prev, g, wup, cw, cb, wdn)


def _proj_kernel(x_ref, g_ref, w_ref, hsum_ref, gh_ref, cos_ref, sin_ref, *out_refs, n_rope):
    hn = _rmsnorm(x_ref[...], g_ref[...]).astype(BF16)
    rows = hn.shape[0]
    lane = lax.broadcasted_iota(jnp.int32, (rows, LANES), 1)
    first_half = (lane & (HEAD_DIM - 1)) < (ROT_DIM // 2)
    hsum = hsum_ref[...]
    cos = cos_ref[...]
    sin = sin_ref[...]
    for c in range(w_ref.shape[1] // PROJ_CHUNK):
        y = _dot(hn, w_ref[:, c * PROJ_CHUNK:(c + 1) * PROJ_CHUNK])
        for j in range(PROJ_CHUNK // LANES):
            i = c * (PROJ_CHUNK // LANES) + j
            yt = y[:, j * LANES:(j + 1) * LANES]
            if i >= n_rope // LANES:
                out_refs[1][i - n_rope // LANES] = yt
                continue
            sq = yt * yt
            hi = sq.astype(BF16)
            lo = (sq - hi.astype(F32)).astype(BF16)
            ms = _dot(jnp.concatenate([hi, lo], axis=1), hsum)
            yn = (yt * lax.rsqrt(ms + EPS)) * gh_ref[...]
            partner = jnp.where(first_half, pltpu.roll(yn, LANES - ROT_DIM // 2, 1),
                                pltpu.roll(yn, ROT_DIM // 2, 1))
            out_refs[0][i] = yn * cos + partner * sin


def _project(x, g, w, hsum, gh, cos, sin, *, layer, n_rope):
    nb, t, d = x.shape
    n = w.shape[2]
    rows = min(ROW_TILE, t)
    row_in = pl.BlockSpec((None, rows, d), lambda b, i: (b, i, 0))
    tab = pl.BlockSpec((rows, LANES), lambda b, i: (i, 0))
    tiles = [n_rope // LANES] + ([(n - n_rope) // LANES] if n > n_rope else [])
    return pl.pallas_call(
        functools.partial(_proj_kernel, n_rope=n_rope),
        grid=(nb, t // rows),
        in_specs=[row_in, _layer_spec(g, layer), _layer_spec(w, layer), _const_spec(hsum.shape),
                  _layer_spec(gh, layer), tab, tab],
        out_specs=[pl.BlockSpec((None, nt, rows, LANES), lambda b, i: (b, 0, i, 0)) for nt in tiles],
        out_shape=[jax.ShapeDtypeStruct((nb, nt, t, LANES), F32) for nt in tiles],
        compiler_params=_params("parallel", "parallel"),
        name="norm_proj_rope",
    )(x, g, w, hsum, gh, cos, sin)


def _attn_group(q_ref, k_ref, v_ref, og_s, lg_s, tile_start, dil, tile):
    per_res = tile // dil
    nq = min(SPAN, per_res)
    nsub = per_res // nq
    sub0 = tile_start // dil
    qi = lax.broadcasted_iota(jnp.int32, (2 * nq, KEY_ROWS), 0) & (nq - 1)
    kj = lax.broadcasted_iota(jnp.int32, (2 * nq, KEY_ROWS), 1)
    rel = qi - kj
    low = lax.broadcasted_iota(jnp.int32, (nq, LANES), 1) < HEAD_DIM
    ones = jnp.ones((KEY_ROWS, LANES), BF16)

    def rows(start, n):
        return pl.ds(start, n, stride=dil) if dil > 1 else pl.ds(start, n)

    def block(idx, carry):
        r = idx // nsub
        j = idx - r * nsub
        q0 = sub0 + j * nq
        k0 = jnp.maximum(q0 + nq - KEY_ROWS, 0)
        dist = rel + (q0 - k0)
        bias = jnp.where((dist >= 0) & (dist <= SPAN), 0.0, -jnp.inf)
        q_rows = rows(r + dil * (j * nq), nq)
        k_rows = rows(r + dil * k0, KEY_ROWS)
        for p in range(N_PAIRS):
            qp = (q_ref[p, q_rows, :] * (SCALE * LOG2E)).astype(BF16)
            zero = jnp.zeros_like(qp)
            qs = jnp.concatenate([jnp.where(low, qp, zero), jnp.where(low, zero, qp)], axis=0)
            s = lax.dot_general(qs, k_ref[p, k_rows, :].astype(BF16), _NT, preferred_element_type=F32)
            s = s + bias
            m = jnp.max(s, axis=-1, keepdims=True)
            pe = jnp.exp2(s - m).astype(BF16)
            od = _dot(pe, jnp.concatenate([v_ref[p, k_rows, :].astype(BF16), ones], axis=1))
            den = od[:, LANES:]
            o = od[:, :LANES] / den
            lse = m * LN2 + jnp.log(den)
            og_s[p, q_rows, :] = jnp.where(low, o[:nq], o[nq:])
            lg_s[p, q_rows, :] = jnp.where(low, lse[:nq], lse[nq:])
        return carry

    lax.fori_loop(0, dil * nsub, block, 0, unroll=2)


def _attn_prompt_kernel(q_ref, k_ref, v_ref, o_ref, og_s, lg_s, la_s, *, tile):
    group = pl.program_id(2)
    tile_start = pl.program_id(1) * tile
    for gi, dil in enumerate(DILATIONS):
        @pl.when(group == gi)
        def _(gi=gi, dil=dil):
            dst = (o_ref, la_s) if gi == 0 else (og_s, lg_s)
            _attn_group(q_ref, k_ref, v_ref, *dst, tile_start, dil, tile)

    @pl.when(group > 0)
    def _():
        def merge(c, carry):
            rows = pl.ds(pl.multiple_of(c * MERGE_ROWS, MERGE_ROWS), MERGE_ROWS)
            for p in range(N_PAIRS):
                la, lg = la_s[p, rows, :], lg_s[p, rows, :]
                top = jnp.maximum(la, lg)
                ea, eg = jnp.exp(la - top), jnp.exp(lg - top)
                tot = ea + eg
                o_ref[p, rows, :] = (ea * o_ref[p, rows, :] + eg * og_s[p, rows, :]) / tot
                la_s[p, rows, :] = top + jnp.log(tot)
            return carry

        lax.fori_loop(0, tile // MERGE_ROWS, merge, 0)


def _attn_prompt(q, k, v):
    nb, _, t, _ = k.shape
    tile = min(ATT_TILE, t)
    assert tile % (DILATIONS[-1] * SUBLANES) == 0 and t % tile == 0 and t >= DILATIONS[-1] * KEY_ROWS
    whole = pl.BlockSpec((None, N_PAIRS, t, LANES), lambda b, i, g: (b, 0, 0, 0),
                         pipeline_mode=pl.Buffered(1))
    scratch = pltpu.VMEM((N_PAIRS, tile, LANES), F32)
    return pl.pallas_call(
        functools.partial(_attn_prompt_kernel, tile=tile),
        grid=(nb, t // tile, N_GROUPS),
        in_specs=[pl.BlockSpec((None, N_PAIRS, tile, LANES), lambda b, i, g: (b, g, i, 0)), whole, whole],
        out_specs=pl.BlockSpec((None, N_PAIRS, tile, LANES), lambda b, i, g: (b, 0, i, 0)),
        out_shape=jax.ShapeDtypeStruct(k.shape, F32),
        scratch_shapes=[scratch] * 3,
        compiler_params=_params("parallel", "arbitrary", "arbitrary"),
        name="attn_prompt",
    )(q, k, v)


def _attn_sample_kernel(q_ref, kn_ref, vn_ref, kt_ref, vt_ref, o_ref, kn_s, vn_s):
    steps, past = q_ref.shape[0], kt_ref.shape[-1]
    nrow = steps * N_KV_HEADS
    kn_s[...] = jnp.zeros_like(kn_s)
    vn_s[...] = jnp.zeros_like(vn_s)
    kn_s[0:steps, :] = kn_ref[...]
    vn_s[0:steps, :] = vn_ref[...]
    kn = kn_s[...].astype(BF16)
    vn = vn_s[...].astype(BF16)
    kt = kt_ref[...].reshape(KV_DIM, past).astype(BF16)
    vt = vt_ref[...].reshape(KV_DIM, past).astype(BF16)

    row = lax.broadcasted_iota(jnp.int32, (nrow, KV_DIM), 0)
    lane = lax.broadcasted_iota(jnp.int32, (nrow, KV_DIM), 1)
    head_lanes = (lane >> 6) == (row & (N_KV_HEADS - 1))
    q = q_ref[...] * SCALE

    def q_rows(g):
        qg = q[:, g * KV_DIM:(g + 1) * KV_DIM]
        rep = jnp.concatenate(
            [jnp.broadcast_to(qg[s:s + 1, :], (N_KV_HEADS, KV_DIM)) for s in range(steps)], axis=0)
        return jnp.where(head_lanes, rep, 0.0)

    qbd = jnp.concatenate([q_rows(g) for g in range(N_GROUPS)], axis=0).astype(BF16)
    s_win = _dot(qbd, kt)
    s_new = lax.dot_general(qbd, kn, _NT, preferred_element_type=F32)

    def masked(scores, first_pos):
        width = scores.shape[1]
        srow = lax.broadcasted_iota(jnp.int32, (nrow, width), 0) >> 3
        col = lax.broadcasted_iota(jnp.int32, (nrow, width), 1)
        dist = (past - first_pos) + srow - col
        in_cache = col < (past + steps - first_pos)
        out = []
        for g, dil in enumerate(DILATIONS):
            ok = (dist >= 0) & (dist <= SPAN * dil) & ((dist & (dil - 1)) == 0) & in_cache
            out.append(jnp.where(ok, scores[g * nrow:(g + 1) * nrow], -jnp.inf))
        return out

    parts = masked(s_win, 0) + masked(s_new, past)
    top = functools.reduce(jnp.maximum, [jnp.max(s, axis=-1, keepdims=True) for s in parts])
    probs = [jnp.exp(s - top) for s in parts]
    den = functools.reduce(jnp.add, [jnp.sum(p, axis=-1, keepdims=True) for p in probs])
    p_win = jnp.concatenate(probs[:N_GROUPS], axis=0).astype(BF16)
    p_new = jnp.concatenate(probs[N_GROUPS:], axis=0).astype(BF16)
    acc = lax.dot_general(p_win, vt, _NT, preferred_element_type=F32) + _dot(p_new, vn)
    acc = functools.reduce(jnp.add, [acc[g * nrow:(g + 1) * nrow] for g in range(N_GROUPS)])
    acc = jnp.where(head_lanes, acc / den, 0.0)
    o_ref[...] = jnp.sum(acc.reshape(steps, N_KV_HEADS, KV_DIM), axis=1)


def _attn_sample(q, kn, vn, cache_k, cache_v):
    nb, steps, _ = kn.shape
    past, heads, hd = cache_k.shape[1:]
    assert past % LANES == 0 and steps <= SUBLANES, (past, steps)
    views = [jnp.transpose(c, (0, 2, 3, 1)) for c in (cache_k, cache_v)]
    cache_spec = pl.BlockSpec((None, heads, hd, past), lambda b: (b, 0, 0, 0))
    new_spec = pl.BlockSpec((None, steps, KV_DIM), lambda b: (b, 0, 0))
    return pl.pallas_call(
        _attn_sample_kernel,
        grid=(nb,),
        in_specs=[pl.BlockSpec((None, steps, N_GROUPS * KV_DIM), lambda b: (b, 0, 0)),
                  new_spec, new_spec, cache_spec, cache_spec],
        out_specs=new_spec,
        out_shape=jax.ShapeDtypeStruct(kn.shape, F32),
        scratch_shapes=[pltpu.VMEM((SPAN, KV_DIM), F32), pltpu.VMEM((SPAN, KV_DIM), F32)],
        compiler_params=_params("parallel"),
        name="attn_sample",
    )(q, kn, vn, *views)


def _rope_tables(pos):
    half = ROT_DIM // 2
    inv = ROPE_THETA ** (-jnp.arange(half, dtype=F32) * (2.0 / ROT_DIM))
    dlane = np.arange(LANES) % HEAD_DIM
    inv_lane = jnp.where(dlane < ROT_DIM, inv[dlane % half], 0.0)
    ang = pos.astype(F32)[:, None] * inv_lane[None, :]
    sign = np.where(dlane < half, -1.0, 1.0).astype(np.float32)
    return jnp.cos(ang), jnp.sin(ang) * sign


def _trunk(x, pos, conv_a_prev, h0, ffn_prev, w, attend, *, tstride):
    n_a = w["w_in"].shape[0]
    depth = w["w_up"].shape[0]
    cos, sin = _rope_tables(pos)
    hs_new, conv_new, ffn_new = [], [], []
    k = v = None
    for layer in range(depth):
        att, j = None, max(layer - n_a, 0)
        if layer < n_a:
            x, cs, hl = _mixer_a(x, conv_a_prev[layer], h0[layer], w["norm_a"], w["w_in"], w["conv_a_w"],
                                 w["conv_a_b"], w["gate_w"], w["gate_b"], w["lam"], w["w_out"],
                                 layer=layer, tstride=tstride)
            conv_new.append(cs)
            hs_new.append(hl)
        else:
            (q,) = _project(x, w["norm_b"], w["w_q"], w["hsum"], w["q_norm"], cos, sin,
                            layer=j, n_rope=w["w_q"].shape[-1])
            att = attend(q, k, v)
        x, fs = _conv_ffn(x, att, w["w_o"], ffn_prev[layer], w["norm_ffn"], w["w_up"], w["ffn_conv_w"],
                          w["ffn_conv_b"], w["w_down"], layer=layer, att_layer=j, tstride=tstride)
        ffn_new.append(fs)
        if layer == n_a - 1:
            k, v = _project(x, w["norm_kv"], w["w_kv"], w["hsum"], w["k_norm"], cos, sin,
                            layer=0, n_rope=KV_DIM)
    return x, k, v, jnp.stack(hs_new), jnp.stack(conv_new), jnp.stack(ffn_new)


def kernel(x_prompt, x_sample, cache_k, cache_v, state_rglru_h, state_rglru_conv, state_ffn_conv, norm_mix_a, w_in_a, conv_a_w, conv_a_b, gate_r_w, gate_r_b, gate_i_w, gate_i_b, lru_lambda, w_out_a, norm_kv, w_kv, k_norm, norm_mix_b, w_q, q_norm, w_o, norm_ffn, w_ffn_up, ffn_conv_w, ffn_conv_b, w_ffn_down):
    n_a, d = norm_mix_a.shape
    depth, _, dff2 = w_ffn_up.shape
    dff = dff2 // 2
    bp, tp, _ = x_prompt.shape
    bs, ts, _ = x_sample.shape

    head_of = np.arange(LANES) // HEAD_DIM
    w = dict(
        norm_a=norm_mix_a[:, None, :], w_in=w_in_a.astype(BF16),
        conv_a_w=conv_a_w, conv_a_b=conv_a_b[:, None, :],
        gate_w=jnp.concatenate([gate_r_w, gate_i_w], axis=-1).astype(BF16),
        gate_b=jnp.concatenate([gate_r_b, gate_i_b], axis=-1)[:, :, None, :],
        lam=lru_lambda[:, None, :], w_out=w_out_a.astype(BF16),
        norm_kv=norm_kv[None, None, :], w_kv=w_kv.astype(BF16)[None],
        k_norm=jnp.tile(k_norm, LANES // HEAD_DIM)[None, None, :],
        norm_b=norm_mix_b[:, None, :], w_q=w_q.astype(BF16),
        q_norm=jnp.tile(q_norm, (1, LANES // HEAD_DIM))[:, None, :],
        w_o=w_o.astype(BF16), norm_ffn=norm_ffn[:, None, :],
        w_up=w_ffn_up.astype(BF16), ffn_conv_w=ffn_conv_w, ffn_conv_b=ffn_conv_b[:, None, :],
        w_down=w_ffn_down.astype(BF16),
        hsum=jnp.asarray(np.tile((head_of[:, None] == head_of[None, :]) / HEAD_DIM, (2, 1)), BF16),
    )

    y_prompt, k_p, v_p, p_h, p_conv, p_ffn = _trunk(
        x_prompt, jnp.arange(tp),
        jnp.zeros((n_a, bp, CONV_A_WIDTH - 1, d), F32), jnp.zeros((n_a, bp, 1, d), F32),
        jnp.zeros((depth, bp, FFN_CONV_WIDTH - 1, dff), F32), w, _attn_prompt, tstride=1)
    keep = min(WINDOWS[-1], tp)

    def last_rows(a):
        return jnp.swapaxes(a[:, :, tp - keep:], 1, 2).reshape(bp, keep, N_KV_HEADS, HEAD_DIM)

    p_cache_k, p_cache_v = last_rows(k_p), last_rows(v_p)

    def to_tm(a):
        a = jnp.swapaxes(a, -3, -2)
        return a.reshape(a.shape[:-3] + (1, a.shape[-3] * bs, a.shape[-1]))

    def from_tm(a):
        a = a.reshape(a.shape[:-3] + (a.shape[-2] // bs, bs, a.shape[-1]))
        return jnp.swapaxes(a, -3, -2)

    def to_tokens(a):
        a = a.reshape(a.shape[1], ts, bs, LANES)
        return jnp.transpose(a, (2, 1, 0, 3)).reshape(bs, ts, -1)

    def from_tokens(a):
        a = a.reshape(bs, ts, -1, LANES)
        return jnp.transpose(a, (2, 1, 0, 3)).reshape(1, -1, ts * bs, LANES)

    def attend_sample(q, k, v):
        return from_tokens(_attn_sample(to_tokens(q), to_tokens(k), to_tokens(v), cache_k, cache_v))

    pos_s = PAST_LEN + jnp.repeat(jnp.arange(ts), bs)
    y_s, k_s, v_s, s_h, s_conv, s_ffn = _trunk(
        to_tm(x_sample), pos_s, to_tm(state_rglru_conv), state_rglru_h[:, None], to_tm(state_ffn_conv),
        w, attend_sample, tstride=bs)

    return (y_prompt, from_tm(y_s), p_h[:, :, 0, :], p_conv, p_ffn, p_cache_k, p_cache_v,
            s_h[:, 0], from_tm(s_conv), from_tm(s_ffn),
            to_tokens(k_s).reshape(bs, ts, N_KV_HEADS, HEAD_DIM),
            to_tokens(v_s).reshape(bs, ts, N_KV_HEADS, HEAD_DIM))
```

```python
import functools
import math

import numpy as np
import jax
import jax.numpy as jnp
from jax import lax
from jax.experimental import pallas as pl
from jax.experimental.pallas import tpu as pltpu

F32 = jnp.float32
BF16 = jnp.bfloat16

N_LRU_BLOCKS = 4
LRU_C = 8.0
CONV_A_WIDTH = 4
FFN_CONV_WIDTH = 3
HEAD_DIM = 64
N_KV_HEADS = 8
KV_DIM = N_KV_HEADS * HEAD_DIM
WINDOWS = (128, 512, 2048)
DILATIONS = (1, 4, 16)
N_GROUPS = len(WINDOWS)
SPAN = 128
ROT_DIM = HEAD_DIM // 4
ROPE_THETA = 500000.0
PAST_LEN = 2048
EPS = 1e-6
SCALE = HEAD_DIM ** -0.5

LANES = 128
SUBLANES = 8
N_PAIRS = KV_DIM // LANES
ATT_TILE = 2048
MERGE_ROWS = 128
MIXER_SUB = 256
SAMPLE_ENTRIES = 2
LOG2E = 1.4426950408889634
LN2 = 0.6931471805599453
KEY_ROWS = 2 * SPAN
ROW_TILE = 512
FF_CHUNK = 3072
PROJ_CHUNK = 512
VMEM_LIMIT = 56 * 1024 * 1024

_NT = (((1,), (1,)), ((), ()))


def _rmsnorm(x, g):
    ms = jnp.mean(x * x, axis=-1, keepdims=True)
    return (x * lax.rsqrt(ms + EPS)) * g


def _dot(a, b):
    return jnp.dot(a, b, preferred_element_type=F32)


def _sigmoid(x):
    return 1.0 / (1.0 + jnp.exp2(x * (-LOG2E)))


def _gelu(x):
    k1 = -2.0 * math.sqrt(2.0 / math.pi) * LOG2E
    return x / (1.0 + jnp.exp2(x * (k1 + (k1 * 0.044715) * (x * x))))


def _const_spec(shape, single_buffer=False):
    nd = len(shape)
    kw = {"pipeline_mode": pl.Buffered(1)} if single_buffer else {}
    return pl.BlockSpec(shape, lambda *_: (0,) * nd, **kw)


def _layer_spec(stacked, layer, single_buffer=False):
    rest = stacked.shape[1:]
    kw = {"pipeline_mode": pl.Buffered(1)} if single_buffer else {}
    return pl.BlockSpec((None,) + rest, lambda *_: (layer,) + (0,) * len(rest), **kw)


def _params(*sem):
    return pltpu.CompilerParams(dimension_semantics=sem, vmem_limit_bytes=VMEM_LIMIT)


def _conv_pad(width, tstride):
    hist = (width - 1) * tstride
    return hist, max(SUBLANES, hist)


def _causal_conv(cat_ref, cols, w_ref, b_ref, width, tstride, rows):
    hist, pad = _conv_pad(width, tstride)
    acc = None
    for j in range(width):
        start = pad - hist + j * tstride
        term = cat_ref[start:start + rows, cols] * w_ref[j:j + 1, cols]
        acc = term if acc is None else acc + term
    return b_ref[:, cols] + acc


def _mixer_a_kernel(*refs, tstride, rows, sub):
    segmented = tstride == 1
    perm_ref = None
    if segmented:
        x_ref, cprev_ref, h0_ref, perm_ref = refs[:4]
        refs = refs[4:]
    else:
        x_ref, cprev_ref, h0_ref = refs[:3]
        refs = refs[3:]
    params = refs[:8]
    xo_ref, cst_ref, hl_ref, hc_s, ch_s = refs[8:13]
    nsub = rows // sub
    work = refs[13:]

    @pl.when(pl.program_id(1) == 0)
    def _():
        ch_s[...] = cprev_ref[...]
        hc_s[...] = h0_ref[...]

    for t in range(nsub):
        part = pl.ds(t * sub, sub)
        cat_s, a_s, u_s = work[3 * t:3 * t + 3]
        _mixer_a_tile(x_ref.at[part], perm_ref, *params, xo_ref.at[part], cst_ref, hl_ref,
                      cat_s, hc_s, a_s, u_s, ch_s,
                      segmented=segmented, seg=SUBLANES if segmented else tstride)


def _mixer_a_tile(x_ref, perm_ref, g_ref, win_ref, cw_ref, cb_ref, gw_ref, gb_ref, lam_ref, wout_ref,
                  xo_ref, cst_ref, hl_ref, cat_s, hc_s, a_s, u_s, ch_s, *, segmented, seg):
    rows, d = x_ref.shape
    blk = d // N_LRU_BLOCKS
    steps = rows // seg
    taps = CONV_A_WIDTH - 1
    hist, pad = _conv_pad(CONV_A_WIDTH, seg)

    x = x_ref[...]
    hn = _rmsnorm(x, g_ref[...]).astype(BF16)
    if segmented:
        hn = _dot(perm_ref[0], hn).astype(BF16)
    y = _dot(hn, win_ref[...])
    gate = y[:, :d]
    cat_s[pad:pad + rows, :] = y[:, d:]
    if segmented:
        tail = cat_s[pad + rows - hist:pad + rows, :].reshape(taps, seg, d)
        first = lax.broadcasted_iota(jnp.int32, tail.shape, 1) == 0
        carried = ch_s[...]
        cat_s[pad - hist:pad, :] = jnp.where(first, carried[:, None, :], pltpu.roll(tail, 1, 1)).reshape(hist, d)
        for k in range(taps):
            ch_s[k:k + 1, :] = cat_s[pad + rows - hist + k * seg + seg - 1:pad + rows - hist + (k + 1) * seg, :]
    else:
        cat_s[pad - hist:pad, :] = ch_s[...]
        ch_s[...] = cat_s[pad + rows - hist:pad + rows, :]
    cst_ref[...] = ch_s[...]
    xc = _causal_conv(cat_s, slice(None), cw_ref, cb_ref, CONV_A_WIDTH, seg, rows)

    lam = lam_ref[...]
    nsp = -lam
    softplus = jnp.maximum(nsp, 0.0) + jnp.log1p(jnp.exp(-jnp.abs(nsp)))
    decay = (-LRU_C * LOG2E) * softplus
    xcb = xc.astype(BF16)
    for n in range(N_LRU_BLOCKS):
        cols = slice(n * blk, (n + 1) * blk)
        ri = _dot(xcb[:, cols], gw_ref[n]) + gb_ref[n]
        r = _sigmoid(ri[:, :blk])
        i = _sigmoid(ri[:, blk:])
        a = jnp.exp2(r * decay[:, cols])
        a_s[:, cols] = a
        gain2 = 1.0 - a * a
        gain = jnp.where(gain2 > 0.0, gain2 * lax.rsqrt(gain2), 0.0)
        u_s[:, cols] = gain * (i * xc[:, cols])

    if segmented:
        def step(j, carry):
            h, c = carry
            r0 = pl.multiple_of(j * seg, seg)
            a = a_s[pl.ds(r0, seg), :]
            h = a * h + u_s[pl.ds(r0, seg), :]
            c = a * c
            u_s[pl.ds(r0, seg), :] = h
            a_s[pl.ds(r0, seg), :] = c
            return h, c

        h_end, c_end = lax.fori_loop(0, steps, step, (jnp.zeros((seg, d), F32), jnp.ones((seg, d), F32)),
                                     unroll=True)
        state, starts = hc_s[...], []
        for s in range(seg):
            starts.append(state)
            state = c_end[s:s + 1, :] * state + h_end[s:s + 1, :]
        start = jnp.concatenate(starts, axis=0)
        hs = (u_s[...].reshape(steps, seg, d) + a_s[...].reshape(steps, seg, d) * start[None]).reshape(rows, d)
        h_last = state
    else:
        h_last = hc_s[...]
        for t in range(steps):
            sl = slice(t * seg, (t + 1) * seg)
            h_last = a_s[sl, :] * h_last + u_s[sl, :]
            u_s[sl, :] = h_last
        hs = u_s[...]
    hc_s[...] = h_last
    hl_ref[...] = h_last

    yv = (hs * _gelu(gate)).astype(BF16)
    if segmented:
        yv = _dot(perm_ref[1], yv).astype(BF16)
    xo_ref[...] = x + _dot(yv, wout_ref[...])


def _mixer_a(x, cprev, h0, g, win, cw, cb, gw, gb, lam, wout, *, layer, tstride):
    nb, t, d = x.shape
    ls = functools.partial(_layer_spec, layer=layer)
    rows = min(ROW_TILE, t)
    hist = (CONV_A_WIDTH - 1) * tstride
    seg = SUBLANES if tstride == 1 else tstride
    _, pad = _conv_pad(CONV_A_WIDTH, seg)
    row_spec = pl.BlockSpec((None, rows, d), lambda b, i: (b, i, 0))
    per_b = lambda n: pl.BlockSpec((None, n, d), lambda b, i: (b, 0, 0))
    perm_args, perm_specs = (), []
    sub = rows
    if tstride == 1:
        sub = min(MIXER_SUB, rows)
        time_of_row = (np.arange(sub) % seg) * (sub // seg) + np.arange(sub) // seg
        perm = np.zeros((sub, sub), np.float32)
        perm[np.arange(sub), time_of_row] = 1.0
        perm_args = (jnp.asarray(np.stack([perm, perm.T]), BF16),)
        perm_specs = [_const_spec((2, sub, sub))]
    return pl.pallas_call(
        functools.partial(_mixer_a_kernel, tstride=tstride, rows=rows, sub=sub),
        grid=(nb, t // rows),
        in_specs=[row_spec, per_b(hist), per_b(tstride)] + perm_specs
        + [ls(g), ls(win, single_buffer=True), ls(cw), ls(cb), ls(gw, single_buffer=True), ls(gb),
           ls(lam), ls(wout, single_buffer=True)],
        out_specs=[row_spec, per_b(hist), per_b(tstride)],
        out_shape=[jax.ShapeDtypeStruct(x.shape, F32),
                   jax.ShapeDtypeStruct((nb, hist, d), F32),
                   jax.ShapeDtypeStruct((nb, tstride, d), F32)],
        scratch_shapes=[pltpu.VMEM((tstride, d), F32), pltpu.VMEM((hist, d), F32)]
        + [pltpu.VMEM((pad + sub, d), F32), pltpu.VMEM((sub, d), F32), pltpu.VMEM((sub, d), F32)] * (rows // sub),
        compiler_params=_params("parallel", "arbitrary"),
        name="mixer_a",
    )(x, cprev, h0, *perm_args, g, win, cw, cb, gw, gb, lam, wout)


def _ffn_kernel(*refs, tstride, rows, has_att):
    x_ref = refs[0]
    if has_att:
        att_ref, wo_ref = refs[1:3]
    prev_ref, g_ref, wup_ref, cw_ref, cb_ref, wdn_ref, xo_ref, st_ref, cat_s = refs[1 + 2 * has_att:]
    dff = wdn_ref.shape[0]
    hist, pad = _conv_pad(FFN_CONV_WIDTH, tstride)

    @pl.when(pl.program_id(1) == 0)
    def _():
        cat_s[pad - hist:pad, :] = prev_ref[...]

    x = x_ref[...]
    if has_att:
        merged = jnp.concatenate([att_ref[p] for p in range(att_ref.shape[0])], axis=1)
        x = x + _dot(merged.astype(BF16), wo_ref[...])

    hn = _rmsnorm(x, g_ref[...]).astype(BF16)
    acc = None
    for c in range(dff // FF_CHUNK):
        cols = slice(c * FF_CHUNK, (c + 1) * FF_CHUNK)
        vcols = slice(dff + c * FF_CHUNK, dff + (c + 1) * FF_CHUNK)
        cat_s[pad:pad + rows, cols] = _dot(hn, wup_ref[:, cols])
        gc = _causal_conv(cat_s, cols, cw_ref, cb_ref, FFN_CONV_WIDTH, tstride, rows)
        act = (_gelu(gc) * _dot(hn, wup_ref[:, vcols])).astype(BF16)
        part = _dot(act, wdn_ref[cols, :])
        acc = part if acc is None else acc + part
    new_hist = cat_s[pad + rows - hist:pad + rows, :]
    cat_s[pad - hist:pad, :] = new_hist
    st_ref[...] = new_hist
    xo_ref[...] = x + acc


def _conv_ffn(x, att, wo, prev, g, wup, cw, cb, wdn, *, layer, att_layer, tstride):
    nb, t, d = x.shape
    dff = wdn.shape[1]
    ls = functools.partial(_layer_spec, layer=layer)
    rows = min(ROW_TILE, t)
    hist, pad = _conv_pad(FFN_CONV_WIDTH, tstride)
    row_spec = pl.BlockSpec((None, rows, d), lambda b, i: (b, i, 0))
    st_spec = pl.BlockSpec((None, hist, dff), lambda b, i: (b, 0, 0))
    has_att = att is not None
    att_args, att_specs = (), []
    if has_att:
        att_args = (att, wo)
        att_specs = [pl.BlockSpec((None, N_PAIRS, rows, LANES), lambda b, i: (b, 0, i, 0)),
                     _layer_spec(wo, att_layer)]
    return pl.pallas_call(
        functools.partial(_ffn_kernel, tstride=tstride, rows=rows, has_att=has_att),
        grid=(nb, t // rows),
        in_specs=[row_spec] + att_specs
        + [st_spec, ls(g), ls(wup, single_buffer=True), ls(cw), ls(cb), ls(wdn, single_buffer=True)],
        out_specs=[row_spec, st_spec],
        out_shape=[jax.ShapeDtypeStruct(x.shape, F32), jax.ShapeDtypeStruct((nb, hist, dff), F32)],
        scratch_shapes=[pltpu.VMEM((pad + rows, dff), F32)],
        compiler_params=_params("parallel", "arbitrary"),
        name="conv_ffn",
    )(x, *att_args, prev, g, wup, cw, cb, wdn)


def _proj_kernel(x_ref, g_ref, w_ref, hsum_ref, gh_ref, cos_ref, sin_ref, *out_refs, n_rope):
    hn = _rmsnorm(x_ref[...], g_ref[...]).astype(BF16)
    rows = hn.shape[0]
    lane = lax.broadcasted_iota(jnp.int32, (rows, LANES), 1)
    first_half = (lane & (HEAD_DIM - 1)) < (ROT_DIM // 2)
    hsum = hsum_ref[...]
    cos = cos_ref[...]
    sin = sin_ref[...]
    for c in range(w_ref.shape[1] // PROJ_CHUNK):
        y = _dot(hn, w_ref[:, c * PROJ_CHUNK:(c + 1) * PROJ_CHUNK])
        for j in range(PROJ_CHUNK // LANES):
            i = c * (PROJ_CHUNK // LANES) + j
            yt = y[:, j * LANES:(j + 1) * LANES]
            if i >= n_rope // LANES:
                out_refs[1][i - n_rope // LANES] = yt
                continue
            sq = yt * yt
            hi = sq.astype(BF16)
            lo = (sq - hi.astype(F32)).astype(BF16)
            ms = _dot(jnp.concatenate([hi, lo], axis=1), hsum)
            yn = (yt * lax.rsqrt(ms + EPS)) * gh_ref[...]
            partner = jnp.where(first_half, pltpu.roll(yn, LANES - ROT_DIM // 2, 1),
                                pltpu.roll(yn, ROT_DIM // 2, 1))
            out_refs[0][i] = yn * cos + partner * sin


def _project(x, g, w, hsum, gh, cos, sin, *, layer, n_rope):
    nb, t, d = x.shape
    n = w.shape[2]
    rows = min(ROW_TILE, t)
    row_in = pl.BlockSpec((None, rows, d), lambda b, i: (b, i, 0))
    tab = pl.BlockSpec((rows, LANES), lambda b, i: (i, 0))
    tiles = [n_rope // LANES] + ([(n - n_rope) // LANES] if n > n_rope else [])
    return pl.pallas_call(
        functools.partial(_proj_kernel, n_rope=n_rope),
        grid=(nb, t // rows),
        in_specs=[row_in, _layer_spec(g, layer), _layer_spec(w, layer), _const_spec(hsum.shape),
                  _layer_spec(gh, layer), tab, tab],
        out_specs=[pl.BlockSpec((None, nt, rows, LANES), lambda b, i: (b, 0, i, 0)) for nt in tiles],
        out_shape=[jax.ShapeDtypeStruct((nb, nt, t, LANES), F32) for nt in tiles],
        compiler_params=_params("parallel", "parallel"),
        name="norm_proj_rope",
    )(x, g, w, hsum, gh, cos, sin)


def _attn_group(q_ref, k_ref, v_ref, og_s, lg_s, tile_start, dil, tile):
    per_res = tile // dil
    nq = min(SPAN, per_res)
    nsub = per_res // nq
    sub0 = tile_start // dil
    qi = lax.broadcasted_iota(jnp.int32, (2 * nq, KEY_ROWS), 0) & (nq - 1)
    kj = lax.broadcasted_iota(jnp.int32, (2 * nq, KEY_ROWS), 1)
    rel = qi - kj
    low = lax.broadcasted_iota(jnp.int32, (nq, LANES), 1) < HEAD_DIM
    ones = jnp.ones((KEY_ROWS, LANES), BF16)

    def rows(start, n):
        return pl.ds(start, n, stride=dil) if dil > 1 else pl.ds(start, n)

    def block(idx, carry):
        r = idx // nsub
        j = idx - r * nsub
        q0 = sub0 + j * nq
        k0 = jnp.maximum(q0 + nq - KEY_ROWS, 0)
        dist = rel + (q0 - k0)
        bias = jnp.where((dist >= 0) & (dist <= SPAN), 0.0, -jnp.inf)
        q_rows = rows(r + dil * (j * nq), nq)
        k_rows = rows(r + dil * k0, KEY_ROWS)
        for p in range(N_PAIRS):
            qp = (q_ref[p, q_rows, :] * (SCALE * LOG2E)).astype(BF16)
            zero = jnp.zeros_like(qp)
            qs = jnp.concatenate([jnp.where(low, qp, zero), jnp.where(low, zero, qp)], axis=0)
            s = lax.dot_general(qs, k_ref[p, k_rows, :].astype(BF16), _NT, preferred_element_type=F32)
            s = s + bias
            m = jnp.max(s, axis=-1, keepdims=True)
            pe = jnp.exp2(s - m).astype(BF16)
            od = _dot(pe, jnp.concatenate([v_ref[p, k_rows, :].astype(BF16), ones], axis=1))
            den = od[:, LANES:]
            o = od[:, :LANES] / den
            lse = m * LN2 + jnp.log(den)
            og_s[p, q_rows, :] = jnp.where(low, o[:nq], o[nq:])
            lg_s[p, q_rows, :] = jnp.where(low, lse[:nq], lse[nq:])
        return carry

    lax.fori_loop(0, dil * nsub, block, 0, unroll=2)


def _attn_prompt_kernel(q_ref, k_ref, v_ref, o_ref, og_s, lg_s, la_s, *, tile):
    group = pl.program_id(2)
    tile_start = pl.program_id(1) * tile
    for gi, dil in enumerate(DILATIONS):
        @pl.when(group == gi)
        def _(gi=gi, dil=dil):
            dst = (o_ref, la_s) if gi == 0 else (og_s, lg_s)
            _attn_group(q_ref, k_ref, v_ref, *dst, tile_start, dil, tile)

    @pl.when(group > 0)
    def _():
        def merge(c, carry):
            rows = pl.ds(pl.multiple_of(c * MERGE_ROWS, MERGE_ROWS), MERGE_ROWS)
            for p in range(N_PAIRS):
                la, lg = la_s[p, rows, :], lg_s[p, rows, :]
                top = jnp.maximum(la, lg)
                ea, eg = jnp.exp(la - top), jnp.exp(lg - top)
                tot = ea + eg
                o_ref[p, rows, :] = (ea * o_ref[p, rows, :] + eg * og_s[p, rows, :]) / tot
                la_s[p, rows, :] = top + jnp.log(tot)
            return carry

        lax.fori_loop(0, tile // MERGE_ROWS, merge, 0)


def _attn_prompt(q, k, v):
    nb, _, t, _ = k.shape
    tile = min(ATT_TILE, t)
    assert tile % (DILATIONS[-1] * SUBLANES) == 0 and t % tile == 0 and t >= DILATIONS[-1] * KEY_ROWS
    whole = pl.BlockSpec((None, N_PAIRS, t, LANES), lambda b, i, g: (b, 0, 0, 0),
                         pipeline_mode=pl.Buffered(1))
    scratch = pltpu.VMEM((N_PAIRS, tile, LANES), F32)
    return pl.pallas_call(
        functools.partial(_attn_prompt_kernel, tile=tile),
        grid=(nb, t // tile, N_GROUPS),
        in_specs=[pl.BlockSpec((None, N_PAIRS, tile, LANES), lambda b, i, g: (b, g, i, 0)), whole, whole],
        out_specs=pl.BlockSpec((None, N_PAIRS, tile, LANES), lambda b, i, g: (b, 0, i, 0)),
        out_shape=jax.ShapeDtypeStruct(k.shape, F32),
        scratch_shapes=[scratch] * 3,
        compiler_params=_params("parallel", "arbitrary", "arbitrary"),
        name="attn_prompt",
    )(q, k, v)


def _attn_sample_kernel(q_ref, kn_ref, vn_ref, kt_ref, vt_ref, o_ref, kn_s, vn_s):
    for i in range(q_ref.shape[0]):
        _attn_sample_entry(q_ref.at[i], kn_ref.at[i], vn_ref.at[i], kt_ref.at[i], vt_ref.at[i],
                           o_ref.at[i], kn_s, vn_s)


def _attn_sample_entry(q_ref, kn_ref, vn_ref, kt_ref, vt_ref, o_ref, kn_s, vn_s):
    steps, past = q_ref.shape[0], kt_ref.shape[-1]
    nrow = steps * N_KV_HEADS
    kn_s[...] = jnp.zeros_like(kn_s)
    vn_s[...] = jnp.zeros_like(vn_s)
    kn_s[0:steps, :] = kn_ref[...]
    vn_s[0:steps, :] = vn_ref[...]
    kn = kn_s[...].astype(BF16)
    vn = vn_s[...].astype(BF16)
    kt = kt_ref[...].reshape(KV_DIM, past).astype(BF16)
    vt = vt_ref[...].reshape(KV_DIM, past).astype(BF16)

    row = lax.broadcasted_iota(jnp.int32, (nrow, KV_DIM), 0)
    lane = lax.broadcasted_iota(jnp.int32, (nrow, KV_DIM), 1)
    head_lanes = (lane >> 6) == (row & (N_KV_HEADS - 1))
    q = q_ref[...] * SCALE

    def q_rows(g):
        qg = q[:, g * KV_DIM:(g + 1) * KV_DIM]
        rep = jnp.concatenate(
            [jnp.broadcast_to(qg[s:s + 1, :], (N_KV_HEADS, KV_DIM)) for s in range(steps)], axis=0)
        return jnp.where(head_lanes, rep, 0.0)

    qbd = jnp.concatenate([q_rows(g) for g in range(N_GROUPS)], axis=0).astype(BF16)
    s_win = _dot(qbd, kt)
    s_new = lax.dot_general(qbd, kn, _NT, preferred_element_type=F32)

    def masked(scores, first_pos):
        width = scores.shape[1]
        srow = lax.broadcasted_iota(jnp.int32, (nrow, width), 0) >> 3
        col = lax.broadcasted_iota(jnp.int32, (nrow, width), 1)
        dist = (past - first_pos) + srow - col
        in_cache = col < (past + steps - first_pos)
        out = []
        for g, dil in enumerate(DILATIONS):
            ok = (dist >= 0) & (dist <= SPAN * dil) & ((dist & (dil - 1)) == 0) & in_cache
            out.append(jnp.where(ok, scores[g * nrow:(g + 1) * nrow], -jnp.inf))
        return out

    parts = masked(s_win, 0) + masked(s_new, past)
    top = functools.reduce(jnp.maximum, [jnp.max(s, axis=-1, keepdims=True) for s in parts])
    probs = [jnp.exp(s - top) for s in parts]
    den = functools.reduce(jnp.add, [jnp.sum(p, axis=-1, keepdims=True) for p in probs])
    p_win = jnp.concatenate(probs[:N_GROUPS], axis=0).astype(BF16)
    p_new = jnp.concatenate(probs[N_GROUPS:], axis=0).astype(BF16)
    acc = lax.dot_general(p_win, vt, _NT, preferred_element_type=F32) + _dot(p_new, vn)
    acc = functools.reduce(jnp.add, [acc[g * nrow:(g + 1) * nrow] for g in range(N_GROUPS)])
    acc = jnp.where(head_lanes, acc / den, 0.0)
    o_ref[...] = jnp.sum(acc.reshape(steps, N_KV_HEADS, KV_DIM), axis=1)


def _attn_sample(q, kn, vn, cache_k, cache_v):
    nb, steps, _ = kn.shape
    past, heads, hd = cache_k.shape[1:]
    assert past % LANES == 0 and steps <= SUBLANES, (past, steps)
    views = [jnp.transpose(c, (0, 2, 3, 1)) for c in (cache_k, cache_v)]
    per = SAMPLE_ENTRIES if nb % SAMPLE_ENTRIES == 0 else 1
    cache_spec = pl.BlockSpec((per, heads, hd, past), lambda b: (b, 0, 0, 0))
    new_spec = pl.BlockSpec((per, steps, KV_DIM), lambda b: (b, 0, 0))
    return pl.pallas_call(
        _attn_sample_kernel,
        grid=(nb // per,),
        in_specs=[pl.BlockSpec((per, steps, N_GROUPS * KV_DIM), lambda b: (b, 0, 0)),
                  new_spec, new_spec, cache_spec, cache_spec],
        out_specs=new_spec,
        out_shape=jax.ShapeDtypeStruct(kn.shape, F32),
        scratch_shapes=[pltpu.VMEM((SPAN, KV_DIM), F32), pltpu.VMEM((SPAN, KV_DIM), F32)],
        compiler_params=_params("parallel"),
        name="attn_sample",
    )(q, kn, vn, *views)


def _rope_tables(pos):
    half = ROT_DIM // 2
    inv = ROPE_THETA ** (-np.arange(half, dtype=np.float64) * (2.0 / ROT_DIM))
    dlane = np.arange(LANES) % HEAD_DIM
    inv_lane = np.where(dlane < ROT_DIM, inv[dlane % half], 0.0)
    ang = np.asarray(pos, np.float64)[:, None] * inv_lane[None, :]
    sign = np.where(dlane < half, -1.0, 1.0)
    return jnp.asarray(np.cos(ang), F32), jnp.asarray(np.sin(ang) * sign, F32)


def _trunk(x, pos, conv_a_prev, h0, ffn_prev, w, attend, *, tstride):
    n_a = w["w_in"].shape[0]
    depth = w["w_up"].shape[0]
    cos, sin = _rope_tables(pos)
    hs_new, conv_new, ffn_new = [], [], []
    k = v = None
    for layer in range(depth):
        att, j = None, max(layer - n_a, 0)
        if layer < n_a:
            x, cs, hl = _mixer_a(x, conv_a_prev[layer], h0[layer], w["norm_a"], w["w_in"], w["conv_a_w"],
                                 w["conv_a_b"], w["gate_w"], w["gate_b"], w["lam"], w["w_out"],
                                 layer=layer, tstride=tstride)
            conv_new.append(cs)
            hs_new.append(hl)
        else:
            (q,) = _project(x, w["norm_b"], w["w_q"], w["hsum"], w["q_norm"], cos, sin,
                            layer=j, n_rope=w["w_q"].shape[-1])
            att = attend(q, k, v)
        x, fs = _conv_ffn(x, att, w["w_o"], ffn_prev[layer], w["norm_ffn"], w["w_up"], w["ffn_conv_w"],
                          w["ffn_conv_b"], w["w_down"], layer=layer, att_layer=j, tstride=tstride)
        ffn_new.append(fs)
        if layer == n_a - 1:
            k, v = _project(x, w["norm_kv"], w["w_kv"], w["hsum"], w["k_norm"], cos, sin,
                            layer=0, n_rope=KV_DIM)
    return x, k, v, jnp.stack(hs_new), jnp.stack(conv_new), jnp.stack(ffn_new)


def kernel(x_prompt, x_sample, cache_k, cache_v, state_rglru_h, state_rglru_conv, state_ffn_conv, norm_mix_a, w_in_a, conv_a_w, conv_a_b, gate_r_w, gate_r_b, gate_i_w, gate_i_b, lru_lambda, w_out_a, norm_kv, w_kv, k_norm, norm_mix_b, w_q, q_norm, w_o, norm_ffn, w_ffn_up, ffn_conv_w, ffn_conv_b, w_ffn_down):
    n_a, d = norm_mix_a.shape
    depth, _, dff2 = w_ffn_up.shape
    dff = dff2 // 2
    bp, tp, _ = x_prompt.shape
    bs, ts, _ = x_sample.shape

    head_of = np.arange(LANES) // HEAD_DIM
    w = dict(
        norm_a=norm_mix_a[:, None, :], w_in=w_in_a.astype(BF16),
        conv_a_w=conv_a_w, conv_a_b=conv_a_b[:, None, :],
        gate_w=jnp.concatenate([gate_r_w, gate_i_w], axis=-1).astype(BF16),
        gate_b=jnp.concatenate([gate_r_b, gate_i_b], axis=-1)[:, :, None, :],
        lam=lru_lambda[:, None, :], w_out=w_out_a.astype(BF16),
        norm_kv=norm_kv[None, None, :], w_kv=w_kv.astype(BF16)[None],
        k_norm=jnp.tile(k_norm, LANES // HEAD_DIM)[None, None, :],
        norm_b=norm_mix_b[:, None, :], w_q=w_q.astype(BF16),
        q_norm=jnp.tile(q_norm, (1, LANES // HEAD_DIM))[:, None, :],
        w_o=w_o.astype(BF16), norm_ffn=norm_ffn[:, None, :],
        w_up=w_ffn_up.astype(BF16), ffn_conv_w=ffn_conv_w, ffn_conv_b=ffn_conv_b[:, None, :],
        w_down=w_ffn_down.astype(BF16),
        hsum=jnp.asarray(np.tile((head_of[:, None] == head_of[None, :]) / HEAD_DIM, (2, 1)), BF16),
    )

    y_prompt, k_p, v_p, p_h, p_conv, p_ffn = _trunk(
        x_prompt, np.arange(tp),
        jnp.zeros((n_a, bp, CONV_A_WIDTH - 1, d), F32), jnp.zeros((n_a, bp, 1, d), F32),
        jnp.zeros((depth, bp, FFN_CONV_WIDTH - 1, dff), F32), w, _attn_prompt, tstride=1)
    keep = min(WINDOWS[-1], tp)

    def last_rows(a):
        return jnp.swapaxes(a[:, :, tp - keep:], 1, 2).reshape(bp, keep, N_KV_HEADS, HEAD_DIM)

    p_cache_k, p_cache_v = last_rows(k_p), last_rows(v_p)

    def to_tm(a):
        a = jnp.swapaxes(a, -3, -2)
        return a.reshape(a.shape[:-3] + (1, a.shape[-3] * bs, a.shape[-1]))

    def from_tm(a):
        a = a.reshape(a.shape[:-3] + (a.shape[-2] // bs, bs, a.shape[-1]))
        return jnp.swapaxes(a, -3, -2)

    def to_tokens(a):
        a = a.reshape(a.shape[1], ts, bs, LANES)
        return jnp.transpose(a, (2, 1, 0, 3)).reshape(bs, ts, -1)

    def from_tokens(a):
        a = a.reshape(bs, ts, -1, LANES)
        return jnp.transpose(a, (2, 1, 0, 3)).reshape(1, -1, ts * bs, LANES)

    def attend_sample(q, k, v):
        return from_tokens(_attn_sample(to_tokens(q), to_tokens(k), to_tokens(v), cache_k, cache_v))

    pos_s = PAST_LEN + np.repeat(np.arange(ts), bs)
    y_s, k_s, v_s, s_h, s_conv, s_ffn = _trunk(
        to_tm(x_sample), pos_s, to_tm(state_rglru_conv), state_rglru_h[:, None], to_tm(state_ffn_conv),
        w, attend_sample, tstride=bs)

    return (y_prompt, from_tm(y_s), p_h[:, :, 0, :], p_conv, p_ffn, p_cache_k, p_cache_v,
            s_h[:, 0], from_tm(s_conv), from_tm(s_ffn),
            to_tokens(k_s).reshape(bs, ts, N_KV_HEADS, HEAD_DIM),
            to_tokens(v_s).reshape(bs, ts, N_KV_HEADS, HEAD_DIM))
```

```python
import functools
import math

import numpy as np
import jax
import jax.numpy as jnp
from jax import lax
from jax.experimental import pallas as pl
from jax.experimental.pallas import tpu as pltpu

F32 = jnp.float32
BF16 = jnp.bfloat16

N_LRU_BLOCKS = 4
LRU_C = 8.0
CONV_A_WIDTH = 4
FFN_CONV_WIDTH = 3
HEAD_DIM = 64
N_KV_HEADS = 8
KV_DIM = N_KV_HEADS * HEAD_DIM
WINDOWS = (128, 512, 2048)
DILATIONS = (1, 4, 16)
N_GROUPS = len(WINDOWS)
SPAN = 128
ROT_DIM = HEAD_DIM // 4
ROPE_THETA = 500000.0
PAST_LEN = 2048
EPS = 1e-6
SCALE = HEAD_DIM ** -0.5

LANES = 128
SUBLANES = 8
N_PAIRS = KV_DIM // LANES
ATT_TILE = 2048
MERGE_ROWS = 128
MIXER_AHEAD_COLS = 256
MIXER_SUB = 256
SAMPLE_ENTRIES = 2
LOG2E = 1.4426950408889634
LN2 = 0.6931471805599453
KEY_ROWS = 2 * SPAN
ROW_TILE = 512
FF_CHUNK = 3072
PROJ_CHUNK = 512
VMEM_LIMIT = 56 * 1024 * 1024

_NT = (((1,), (1,)), ((), ()))


def _rmsnorm(x, g):
    ms = jnp.mean(x * x, axis=-1, keepdims=True)
    return (x * lax.rsqrt(ms + EPS)) * g


def _dot(a, b):
    return jnp.dot(a, b, preferred_element_type=F32)


def _sigmoid(x):
    return 1.0 / (1.0 + jnp.exp2(x * (-LOG2E)))


def _gelu(x):
    k1 = -2.0 * math.sqrt(2.0 / math.pi) * LOG2E
    return x / (1.0 + jnp.exp2(x * (k1 + (k1 * 0.044715) * (x * x))))


def _const_spec(shape, single_buffer=False):
    nd = len(shape)
    kw = {"pipeline_mode": pl.Buffered(1)} if single_buffer else {}
    return pl.BlockSpec(shape, lambda *_: (0,) * nd, **kw)


def _layer_spec(stacked, layer, single_buffer=False):
    rest = stacked.shape[1:]
    kw = {"pipeline_mode": pl.Buffered(1)} if single_buffer else {}
    return pl.BlockSpec((None,) + rest, lambda *_: (layer,) + (0,) * len(rest), **kw)


def _params(*sem):
    return pltpu.CompilerParams(dimension_semantics=sem, vmem_limit_bytes=VMEM_LIMIT)


def _conv_pad(width, tstride):
    hist = (width - 1) * tstride
    return hist, max(SUBLANES, hist)


def _causal_conv(cat_ref, cols, w_ref, b_ref, width, tstride, rows):
    hist, pad = _conv_pad(width, tstride)
    acc = None
    for j in range(width):
        start = pad - hist + j * tstride
        term = cat_ref[start:start + rows, cols] * w_ref[j:j + 1, cols]
        acc = term if acc is None else acc + term
    return b_ref[:, cols] + acc


def _mixer_a_kernel(*refs, tstride, rows, sub):
    segmented = tstride == 1
    xn_ref = perm_ref = None
    if segmented:
        x_ref, xn_ref, cprev_ref, h0_ref, perm_ref = refs[:5]
        refs = refs[5:]
    else:
        x_ref, cprev_ref, h0_ref = refs[:3]
        refs = refs[3:]
    g_ref, win_ref, cw_ref, cb_ref, gw_ref, gb_ref, lam_ref, wout_ref = refs[:8]
    xo_ref, cst_ref, hl_ref, hc_s, ch_s = refs[8:13]
    nsub = rows // sub
    work = refs[13:13 + 3 * nsub]
    ybuf = refs[13 + 3 * nsub:]
    seg = SUBLANES if segmented else tstride

    @pl.when(pl.program_id(1) == 0)
    def _():
        ch_s[...] = cprev_ref[...]
        hc_s[...] = h0_ref[...]

    def normed(x):
        hn = _rmsnorm(x, g_ref[...]).astype(BF16)
        if segmented:
            hn = jnp.concatenate([_dot(perm_ref[0], hn[t * sub:(t + 1) * sub]).astype(BF16)
                                  for t in range(nsub)], axis=0)
        return hn

    def project_into(x, dst):
        hn = normed(x)
        for c in range(dst.shape[1] // MIXER_AHEAD_COLS):
            cols = slice(c * MIXER_AHEAD_COLS, (c + 1) * MIXER_AHEAD_COLS)
            dst[:, cols] = _dot(hn, win_ref[:, cols])
            yield

    def finish(y, ahead=iter(())):
        yv = []
        for t in range(nsub):
            tile = _mixer_a_tile(y, t * sub, sub, perm_ref, cw_ref, cb_ref, gw_ref, gb_ref, lam_ref, cst_ref,
                                 hl_ref, *work[3 * t:3 * t + 3], hc_s, ch_s, segmented=segmented, seg=seg)
            while True:
                try:
                    next(tile)
                except StopIteration as done:
                    yv.append(done.value)
                    break
                next(ahead, None)
        for _ in ahead:
            pass
        xo_ref[...] = x_ref[...] + _dot(jnp.concatenate(yv, axis=0), wout_ref[...])

    if not segmented:
        finish(_dot(normed(x_ref[...]), win_ref[...]))
        return

    step = pl.program_id(0) * pl.num_programs(1) + pl.program_id(1)

    @pl.when(step == 0)
    def _():
        for _ in project_into(x_ref[...], ybuf[0]):
            pass

    for slot in range(2):
        @pl.when((step & 1) == slot)
        def _(slot=slot):
            finish(ybuf[slot], project_into(xn_ref[...], ybuf[1 - slot]))


def _mixer_a_tile(y, row0, rows, perm_ref, cw_ref, cb_ref, gw_ref, gb_ref, lam_ref, cst_ref, hl_ref,
                  cat_s, a_s, u_s, hc_s, ch_s, *, segmented, seg):
    d = y.shape[1] // 2
    blk = d // N_LRU_BLOCKS
    steps = rows // seg
    taps = CONV_A_WIDTH - 1
    hist, pad = _conv_pad(CONV_A_WIDTH, seg)

    gate = y[row0:row0 + rows, :d]
    cat_s[pad:pad + rows, :] = y[row0:row0 + rows, d:]
    if segmented:
        tail = cat_s[pad + rows - hist:pad + rows, :].reshape(taps, seg, d)
        first = lax.broadcasted_iota(jnp.int32, tail.shape, 1) == 0
        carried = ch_s[...]
        cat_s[pad - hist:pad, :] = jnp.where(first, carried[:, None, :], pltpu.roll(tail, 1, 1)).reshape(hist, d)
        for k in range(taps):
            ch_s[k:k + 1, :] = cat_s[pad + rows - hist + k * seg + seg - 1:pad + rows - hist + (k + 1) * seg, :]
    else:
        cat_s[pad - hist:pad, :] = ch_s[...]
        ch_s[...] = cat_s[pad + rows - hist:pad + rows, :]
    cst_ref[...] = ch_s[...]
    xc = _causal_conv(cat_s, slice(None), cw_ref, cb_ref, CONV_A_WIDTH, seg, rows)

    lam = lam_ref[...]
    nsp = -lam
    softplus = jnp.maximum(nsp, 0.0) + jnp.log1p(jnp.exp(-jnp.abs(nsp)))
    decay = (-LRU_C * LOG2E) * softplus
    xcb = xc.astype(BF16)
    for n in range(N_LRU_BLOCKS):
        cols = slice(n * blk, (n + 1) * blk)
        ri = _dot(xcb[:, cols], gw_ref[n]) + gb_ref[n]
        r = _sigmoid(ri[:, :blk])
        i = _sigmoid(ri[:, blk:])
        a = jnp.exp2(r * decay[:, cols])
        a_s[:, cols] = a
        gain2 = 1.0 - a * a
        gain = jnp.where(gain2 > 0.0, gain2 * lax.rsqrt(gain2), 0.0)
        u_s[:, cols] = gain * (i * xc[:, cols])
        yield

    if segmented:
        h_end, c_end = jnp.zeros((seg, d), F32), jnp.ones((seg, d), F32)
        for j in range(steps):
            sl = slice(j * seg, (j + 1) * seg)
            a = a_s[sl, :]
            h_end = a * h_end + u_s[sl, :]
            c_end = a * c_end
            u_s[sl, :] = h_end
            a_s[sl, :] = c_end
        state, starts = hc_s[...], []
        for s in range(seg):
            starts.append(state)
            state = c_end[s:s + 1, :] * state + h_end[s:s + 1, :]
        start = jnp.concatenate(starts, axis=0)
        hs = (u_s[...].reshape(steps, seg, d) + a_s[...].reshape(steps, seg, d) * start[None]).reshape(rows, d)
        h_last = state
    else:
        h_last = hc_s[...]
        for t in range(steps):
            sl = slice(t * seg, (t + 1) * seg)
            h_last = a_s[sl, :] * h_last + u_s[sl, :]
            u_s[sl, :] = h_last
        hs = u_s[...]
    hc_s[...] = h_last
    hl_ref[...] = h_last
    yield

    yv = (hs * _gelu(gate)).astype(BF16)
    if segmented:
        yv = _dot(perm_ref[1], yv).astype(BF16)
    return yv


def _mixer_a(x, cprev, h0, g, win, cw, cb, gw, gb, lam, wout, *, layer, tstride):
    nb, t, d = x.shape
    ls = functools.partial(_layer_spec, layer=layer)
    rows = min(ROW_TILE, t)
    hist = (CONV_A_WIDTH - 1) * tstride
    seg = SUBLANES if tstride == 1 else tstride
    _, pad = _conv_pad(CONV_A_WIDTH, seg)
    row_spec = pl.BlockSpec((None, rows, d), lambda b, i: (b, i, 0))
    per_b = lambda n: pl.BlockSpec((None, n, d), lambda b, i: (b, 0, 0))
    tiles = t // rows
    x_args, x_specs, perm_args, perm_specs, ybuf = (x,), [row_spec], (), [], []
    sub, semantics = rows, ("parallel", "arbitrary")
    if tstride == 1:
        x_args = (x, x.reshape(nb * t, d))
        x_specs = [row_spec, pl.BlockSpec((rows, d), lambda b, i: (jnp.minimum(b * tiles + i + 1, nb * tiles - 1), 0))]
        sub = min(MIXER_SUB, rows)
        time_of_row = (np.arange(sub) % seg) * (sub // seg) + np.arange(sub) // seg
        perm = np.zeros((sub, sub), np.float32)
        perm[np.arange(sub), time_of_row] = 1.0
        perm_args = (jnp.asarray(np.stack([perm, perm.T]), BF16),)
        perm_specs = [_const_spec((2, sub, sub))]
        ybuf = [pltpu.VMEM((rows, 2 * d), F32)] * 2
        semantics = ("arbitrary", "arbitrary")
    return pl.pallas_call(
        functools.partial(_mixer_a_kernel, tstride=tstride, rows=rows, sub=sub),
        grid=(nb, tiles),
        in_specs=x_specs + [per_b(hist), per_b(tstride)] + perm_specs
        + [ls(g), ls(win, single_buffer=True), ls(cw), ls(cb), ls(gw, single_buffer=True), ls(gb),
           ls(lam), ls(wout, single_buffer=True)],
        out_specs=[row_spec, per_b(hist), per_b(tstride)],
        out_shape=[jax.ShapeDtypeStruct(x.shape, F32),
                   jax.ShapeDtypeStruct((nb, hist, d), F32),
                   jax.ShapeDtypeStruct((nb, tstride, d), F32)],
        scratch_shapes=[pltpu.VMEM((tstride, d), F32), pltpu.VMEM((hist, d), F32)]
        + [pltpu.VMEM((pad + sub, d), F32), pltpu.VMEM((sub, d), F32), pltpu.VMEM((sub, d), F32)] * (rows // sub)
        + ybuf,
        compiler_params=_params(*semantics),
        name="mixer_a",
    )(*x_args, cprev, h0, *perm_args, g, win, cw, cb, gw, gb, lam, wout)


def _ffn_kernel(*refs, tstride, rows, has_att):
    x_ref = refs[0]
    if has_att:
        att_ref, wo_ref = refs[1:3]
    prev_ref, g_ref, wup_ref, cw_ref, cb_ref, wdn_ref, xo_ref, st_ref, cat_s = refs[1 + 2 * has_att:]
    dff = wdn_ref.shape[0]
    hist, pad = _conv_pad(FFN_CONV_WIDTH, tstride)

    @pl.when(pl.program_id(1) == 0)
    def _():
        cat_s[pad - hist:pad, :] = prev_ref[...]

    x = x_ref[...]
    if has_att:
        merged = jnp.concatenate([att_ref[p] for p in range(att_ref.shape[0])], axis=1)
        x = x + _dot(merged.astype(BF16), wo_ref[...])

    hn = _rmsnorm(x, g_ref[...]).astype(BF16)
    acc = None
    for c in range(dff // FF_CHUNK):
        cols = slice(c * FF_CHUNK, (c + 1) * FF_CHUNK)
        vcols = slice(dff + c * FF_CHUNK, dff + (c + 1) * FF_CHUNK)
        cat_s[pad:pad + rows, cols] = _dot(hn, wup_ref[:, cols])
        gc = _causal_conv(cat_s, cols, cw_ref, cb_ref, FFN_CONV_WIDTH, tstride, rows)
        act = (_gelu(gc) * _dot(hn, wup_ref[:, vcols])).astype(BF16)
        part = _dot(act, wdn_ref[cols, :])
        acc = part if acc is None else acc + part
    new_hist = cat_s[pad + rows - hist:pad + rows, :]
    cat_s[pad - hist:pad, :] = new_hist
    st_ref[...] = new_hist
    xo_ref[...] = x + acc


def _conv_ffn(x, att, wo, prev, g, wup, cw, cb, wdn, *, layer, att_layer, tstride):
    nb, t, d = x.shape
    dff = wdn.shape[1]
    ls = functools.partial(_layer_spec, layer=layer)
    rows = min(ROW_TILE, t)
    hist, pad = _conv_pad(FFN_CONV_WIDTH, tstride)
    row_spec = pl.BlockSpec((None, rows, d), lambda b, i: (b, i, 0))
    st_spec = pl.BlockSpec((None, hist, dff), lambda b, i: (b, 0, 0))
    has_att = att is not None
    att_args, att_specs = (), []
    if has_att:
        att_args = (att, wo)
        att_specs = [pl.BlockSpec((None, N_PAIRS, rows, LANES), lambda b, i: (b, 0, i, 0)),
                     _layer_spec(wo, att_layer)]
    return pl.pallas_call(
        functools.partial(_ffn_kernel, tstride=tstride, rows=rows, has_att=has_att),
        grid=(nb, t // rows),
        in_specs=[row_spec] + att_specs
        + [st_spec, ls(g), ls(wup, single_buffer=True), ls(cw), ls(cb), ls(wdn, single_buffer=True)],
        out_specs=[row_spec, st_spec],
        out_shape=[jax.ShapeDtypeStruct(x.shape, F32), jax.ShapeDtypeStruct((nb, hist, dff), F32)],
        scratch_shapes=[pltpu.VMEM((pad + rows, dff), F32)],
        compiler_params=_params("parallel", "arbitrary"),
        name="conv_ffn",
    )(x, *att_args, prev, g, wup, cw, cb, wdn)


def _proj_kernel(x_ref, g_ref, w_ref, hsum_ref, gh_ref, cos_ref, sin_ref, *out_refs, n_rope):
    hn = _rmsnorm(x_ref[...], g_ref[...]).astype(BF16)
    rows = hn.shape[0]
    lane = lax.broadcasted_iota(jnp.int32, (rows, LANES), 1)
    first_half = (lane & (HEAD_DIM - 1)) < (ROT_DIM // 2)
    hsum = hsum_ref[...]
    cos = cos_ref[...]
    sin = sin_ref[...]
    for c in range(w_ref.shape[1] // PROJ_CHUNK):
        y = _dot(hn, w_ref[:, c * PROJ_CHUNK:(c + 1) * PROJ_CHUNK])
        for j in range(PROJ_CHUNK // LANES):
            i = c * (PROJ_CHUNK // LANES) + j
            yt = y[:, j * LANES:(j + 1) * LANES]
            if i >= n_rope // LANES:
                out_refs[1][i - n_rope // LANES] = yt
                continue
            sq = yt * yt
            hi = sq.astype(BF16)
            lo = (sq - hi.astype(F32)).astype(BF16)
            ms = _dot(jnp.concatenate([hi, lo], axis=1), hsum)
            yn = (yt * lax.rsqrt(ms + EPS)) * gh_ref[...]
            partner = jnp.where(first_half, pltpu.roll(yn, LANES - ROT_DIM // 2, 1),
                                pltpu.roll(yn, ROT_DIM // 2, 1))
            out_refs[0][i] = yn * cos + partner * sin


def _project(x, g, w, hsum, gh, cos, sin, *, layer, n_rope):
    nb, t, d = x.shape
    n = w.shape[2]
    rows = min(ROW_TILE, t)
    row_in = pl.BlockSpec((None, rows, d), lambda b, i: (b, i, 0))
    tab = pl.BlockSpec((rows, LANES), lambda b, i: (i, 0))
    tiles = [n_rope // LANES] + ([(n - n_rope) // LANES] if n > n_rope else [])
    return pl.pallas_call(
        functools.partial(_proj_kernel, n_rope=n_rope),
        grid=(nb, t // rows),
        in_specs=[row_in, _layer_spec(g, layer), _layer_spec(w, layer), _const_spec(hsum.shape),
                  _layer_spec(gh, layer), tab, tab],
        out_specs=[pl.BlockSpec((None, nt, rows, LANES), lambda b, i: (b, 0, i, 0)) for nt in tiles],
        out_shape=[jax.ShapeDtypeStruct((nb, nt, t, LANES), F32) for nt in tiles],
        compiler_params=_params("parallel", "parallel"),
        name="norm_proj_rope",
    )(x, g, w, hsum, gh, cos, sin)


def _attn_group(q_ref, k_ref, v_ref, og_s, lg_s, tile_start, dil, tile):
    per_res = tile // dil
    nq = min(SPAN, per_res)
    nsub = per_res // nq
    sub0 = tile_start // dil
    qi = lax.broadcasted_iota(jnp.int32, (2 * nq, KEY_ROWS), 0) & (nq - 1)
    kj = lax.broadcasted_iota(jnp.int32, (2 * nq, KEY_ROWS), 1)
    rel = qi - kj
    low = lax.broadcasted_iota(jnp.int32, (nq, LANES), 1) < HEAD_DIM
    ones = jnp.ones((KEY_ROWS, LANES), BF16)

    def rows(start, n):
        return pl.ds(start, n, stride=dil) if dil > 1 else pl.ds(start, n)

    def block(idx, carry):
        r = idx // nsub
        j = idx - r * nsub
        q0 = sub0 + j * nq
        k0 = jnp.maximum(q0 + nq - KEY_ROWS, 0)
        dist = rel + (q0 - k0)
        bias = jnp.where((dist >= 0) & (dist <= SPAN), 0.0, -jnp.inf)
        q_rows = rows(r + dil * (j * nq), nq)
        k_rows = rows(r + dil * k0, KEY_ROWS)
        for p in range(N_PAIRS):
            qp = (q_ref[p, q_rows, :] * (SCALE * LOG2E)).astype(BF16)
            zero = jnp.zeros_like(qp)
            qs = jnp.concatenate([jnp.where(low, qp, zero), jnp.where(low, zero, qp)], axis=0)
            s = lax.dot_general(qs, k_ref[p, k_rows, :].astype(BF16), _NT, preferred_element_type=F32)
            s = s + bias
            m = jnp.max(s, axis=-1, keepdims=True)
            pe = jnp.exp2(s - m).astype(BF16)
            od = _dot(pe, jnp.concatenate([v_ref[p, k_rows, :].astype(BF16), ones], axis=1))
            den = od[:, LANES:]
            o = od[:, :LANES] / den
            lse = m * LN2 + jnp.log(den)
            og_s[p, q_rows, :] = jnp.where(low, o[:nq], o[nq:])
            lg_s[p, q_rows, :] = jnp.where(low, lse[:nq], lse[nq:])
        return carry

    lax.fori_loop(0, dil * nsub, block, 0, unroll=2)


def _attn_prompt_kernel(q_ref, k_ref, v_ref, o_ref, og_s, lg_s, la_s, *, tile):
    group = pl.program_id(2)
    tile_start = pl.program_id(1) * tile
    for gi, dil in enumerate(DILATIONS):
        @pl.when(group == gi)
        def _(gi=gi, dil=dil):
            dst = (o_ref, la_s) if gi == 0 else (og_s, lg_s)
            _attn_group(q_ref, k_ref, v_ref, *dst, tile_start, dil, tile)

    @pl.when(group > 0)
    def _():
        def merge(c, carry):
            rows = pl.ds(pl.multiple_of(c * MERGE_ROWS, MERGE_ROWS), MERGE_ROWS)
            for p in range(N_PAIRS):
                la, lg = la_s[p, rows, :], lg_s[p, rows, :]
                top = jnp.maximum(la, lg)
                ea, eg = jnp.exp(la - top), jnp.exp(lg - top)
                tot = ea + eg
                o_ref[p, rows, :] = (ea * o_ref[p, rows, :] + eg * og_s[p, rows, :]) / tot
                la_s[p, rows, :] = top + jnp.log(tot)
            return carry

        lax.fori_loop(0, tile // MERGE_ROWS, merge, 0)


def _attn_prompt(q, k, v):
    nb, _, t, _ = k.shape
    tile = min(ATT_TILE, t)
    assert tile % (DILATIONS[-1] * SUBLANES) == 0 and t % tile == 0 and t >= DILATIONS[-1] * KEY_ROWS
    whole = pl.BlockSpec((None, N_PAIRS, t, LANES), lambda b, i, g: (b, 0, 0, 0),
                         pipeline_mode=pl.Buffered(1))
    scratch = pltpu.VMEM((N_PAIRS, tile, LANES), F32)
    return pl.pallas_call(
        functools.partial(_attn_prompt_kernel, tile=tile),
        grid=(nb, t // tile, N_GROUPS),
        in_specs=[pl.BlockSpec((None, N_PAIRS, tile, LANES), lambda b, i, g: (b, g, i, 0)), whole, whole],
        out_specs=pl.BlockSpec((None, N_PAIRS, tile, LANES), lambda b, i, g: (b, 0, i, 0)),
        out_shape=jax.ShapeDtypeStruct(k.shape, F32),
        scratch_shapes=[scratch] * 3,
        compiler_params=_params("parallel", "arbitrary", "arbitrary"),
        name="attn_prompt",
    )(q, k, v)


def _attn_sample_kernel(q_ref, kn_ref, vn_ref, kt_ref, vt_ref, o_ref, kn_s, vn_s):
    for i in range(q_ref.shape[0]):
        _attn_sample_entry(q_ref.at[i], kn_ref.at[i], vn_ref.at[i], kt_ref.at[i], vt_ref.at[i],
                           o_ref.at[i], kn_s, vn_s)


def _attn_sample_entry(q_ref, kn_ref, vn_ref, kt_ref, vt_ref, o_ref, kn_s, vn_s):
    steps, past = q_ref.shape[0], kt_ref.shape[-1]
    nrow = steps * N_KV_HEADS
    kn_s[...] = jnp.zeros_like(kn_s)
    vn_s[...] = jnp.zeros_like(vn_s)
    kn_s[0:steps, :] = kn_ref[...]
    vn_s[0:steps, :] = vn_ref[...]
    kn = kn_s[...].astype(BF16)
    vn = vn_s[...].astype(BF16)
    kt = kt_ref[...].reshape(KV_DIM, past).astype(BF16)
    vt = vt_ref[...].reshape(KV_DIM, past).astype(BF16)

    row = lax.broadcasted_iota(jnp.int32, (nrow, KV_DIM), 0)
    lane = lax.broadcasted_iota(jnp.int32, (nrow, KV_DIM), 1)
    head_lanes = (lane >> 6) == (row & (N_KV_HEADS - 1))
    q = q_ref[...] * SCALE

    def q_rows(g):
        qg = q[:, g * KV_DIM:(g + 1) * KV_DIM]
        rep = jnp.concatenate(
            [jnp.broadcast_to(qg[s:s + 1, :], (N_KV_HEADS, KV_DIM)) for s in range(steps)], axis=0)
        return jnp.where(head_lanes, rep, 0.0)

    qbd = jnp.concatenate([q_rows(g) for g in range(N_GROUPS)], axis=0).astype(BF16)
    s_win = _dot(qbd, kt)
    s_new = lax.dot_general(qbd, kn, _NT, preferred_element_type=F32)

    def masked(scores, first_pos):
        width = scores.shape[1]
        srow = lax.broadcasted_iota(jnp.int32, (nrow, width), 0) >> 3
        col = lax.broadcasted_iota(jnp.int32, (nrow, width), 1)
        dist = (past - first_pos) + srow - col
        in_cache = col < (past + steps - first_pos)
        out = []
        for g, dil in enumerate(DILATIONS):
            ok = (dist >= 0) & (dist <= SPAN * dil) & ((dist & (dil - 1)) == 0) & in_cache
            out.append(jnp.where(ok, scores[g * nrow:(g + 1) * nrow], -jnp.inf))
        return out

    parts = masked(s_win, 0) + masked(s_new, past)
    top = functools.reduce(jnp.maximum, [jnp.max(s, axis=-1, keepdims=True) for s in parts])
    probs = [jnp.exp(s - top) for s in parts]
    den = functools.reduce(jnp.add, [jnp.sum(p, axis=-1, keepdims=True) for p in probs])
    p_win = jnp.concatenate(probs[:N_GROUPS], axis=0).astype(BF16)
    p_new = jnp.concatenate(probs[N_GROUPS:], axis=0).astype(BF16)
    acc = lax.dot_general(p_win, vt, _NT, preferred_element_type=F32) + _dot(p_new, vn)
    acc = functools.reduce(jnp.add, [acc[g * nrow:(g + 1) * nrow] for g in range(N_GROUPS)])
    acc = jnp.where(head_lanes, acc / den, 0.0)
    o_ref[...] = jnp.sum(acc.reshape(steps, N_KV_HEADS, KV_DIM), axis=1)


def _attn_sample(q, kn, vn, cache_k, cache_v):
    nb, steps, _ = kn.shape
    past, heads, hd = cache_k.shape[1:]
    assert past % LANES == 0 and steps <= SUBLANES, (past, steps)
    views = [jnp.transpose(c, (0, 2, 3, 1)) for c in (cache_k, cache_v)]
    per = SAMPLE_ENTRIES if nb % SAMPLE_ENTRIES == 0 else 1
    cache_spec = pl.BlockSpec((per, heads, hd, past), lambda b: (b, 0, 0, 0))
    new_spec = pl.BlockSpec((per, steps, KV_DIM), lambda b: (b, 0, 0))
    return pl.pallas_call(
        _attn_sample_kernel,
        grid=(nb // per,),
        in_specs=[pl.BlockSpec((per, steps, N_GROUPS * KV_DIM), lambda b: (b, 0, 0)),
                  new_spec, new_spec, cache_spec, cache_spec],
        out_specs=new_spec,
        out_shape=jax.ShapeDtypeStruct(kn.shape, F32),
        scratch_shapes=[pltpu.VMEM((SPAN, KV_DIM), F32), pltpu.VMEM((SPAN, KV_DIM), F32)],
        compiler_params=_params("parallel"),
        name="attn_sample",
    )(q, kn, vn, *views)


def _rope_tables(pos):
    half = ROT_DIM // 2
    inv = ROPE_THETA ** (-np.arange(half, dtype=np.float64) * (2.0 / ROT_DIM))
    dlane = np.arange(LANES) % HEAD_DIM
    inv_lane = np.where(dlane < ROT_DIM, inv[dlane % half], 0.0)
    ang = np.asarray(pos, np.float64)[:, None] * inv_lane[None, :]
    sign = np.where(dlane < half, -1.0, 1.0)
    return jnp.asarray(np.cos(ang), F32), jnp.asarray(np.sin(ang) * sign, F32)


def _trunk(x, pos, conv_a_prev, h0, ffn_prev, w, attend, *, tstride):
    n_a = w["w_in"].shape[0]
    depth = w["w_up"].shape[0]
    cos, sin = _rope_tables(pos)
    hs_new, conv_new, ffn_new = [], [], []
    k = v = None
    for layer in range(depth):
        att, j = None, max(layer - n_a, 0)
        if layer < n_a:
            x, cs, hl = _mixer_a(x, conv_a_prev[layer], h0[layer], w["norm_a"], w["w_in"], w["conv_a_w"],
                                 w["conv_a_b"], w["gate_w"], w["gate_b"], w["lam"], w["w_out"],
                                 layer=layer, tstride=tstride)
            conv_new.append(cs)
            hs_new.append(hl)
        else:
            (q,) = _project(x, w["norm_b"], w["w_q"], w["hsum"], w["q_norm"], cos, sin,
                            layer=j, n_rope=w["w_q"].shape[-1])
            att = attend(q, k, v)
        x, fs = _conv_ffn(x, att, w["w_o"], ffn_prev[layer], w["norm_ffn"], w["w_up"], w["ffn_conv_w"],
                          w["ffn_conv_b"], w["w_down"], layer=layer, att_layer=j, tstride=tstride)
        ffn_new.append(fs)
        if layer == n_a - 1:
            k, v = _project(x, w["norm_kv"], w["w_kv"], w["hsum"], w["k_norm"], cos, sin,
                            layer=0, n_rope=KV_DIM)
    return x, k, v, jnp.stack(hs_new), jnp.stack(conv_new), jnp.stack(ffn_new)


def kernel(x_prompt, x_sample, cache_k, cache_v, state_rglru_h, state_rglru_conv, state_ffn_conv, norm_mix_a, w_in_a, conv_a_w, conv_a_b, gate_r_w, gate_r_b, gate_i_w, gate_i_b, lru_lambda, w_out_a, norm_kv, w_kv, k_norm, norm_mix_b, w_q, q_norm, w_o, norm_ffn, w_ffn_up, ffn_conv_w, ffn_conv_b, w_ffn_down):
    n_a, d = norm_mix_a.shape
    depth, _, dff2 = w_ffn_up.shape
    dff = dff2 // 2
    bp, tp, _ = x_prompt.shape
    bs, ts, _ = x_sample.shape

    head_of = np.arange(LANES) // HEAD_DIM
    w = dict(
        norm_a=norm_mix_a[:, None, :], w_in=w_in_a.astype(BF16),
        conv_a_w=conv_a_w, conv_a_b=conv_a_b[:, None, :],
        gate_w=jnp.concatenate([gate_r_w, gate_i_w], axis=-1).astype(BF16),
        gate_b=jnp.concatenate([gate_r_b, gate_i_b], axis=-1)[:, :, None, :],
        lam=lru_lambda[:, None, :], w_out=w_out_a.astype(BF16),
        norm_kv=norm_kv[None, None, :], w_kv=w_kv.astype(BF16)[None],
        k_norm=jnp.tile(k_norm, LANES // HEAD_DIM)[None, None, :],
        norm_b=norm_mix_b[:, None, :], w_q=w_q.astype(BF16),
        q_norm=jnp.tile(q_norm, (1, LANES // HEAD_DIM))[:, None, :],
        w_o=w_o.astype(BF16), norm_ffn=norm_ffn[:, None, :],
        w_up=w_ffn_up.astype(BF16), ffn_conv_w=ffn_conv_w, ffn_conv_b=ffn_conv_b[:, None, :],
        w_down=w_ffn_down.astype(BF16),
        hsum=jnp.asarray(np.tile((head_of[:, None] == head_of[None, :]) / HEAD_DIM, (2, 1)), BF16),
    )

    y_prompt, k_p, v_p, p_h, p_conv, p_ffn = _trunk(
        x_prompt, np.arange(tp),
        jnp.zeros((n_a, bp, CONV_A_WIDTH - 1, d), F32), jnp.zeros((n_a, bp, 1, d), F32),
        jnp.zeros((depth, bp, FFN_CONV_WIDTH - 1, dff), F32), w, _attn_prompt, tstride=1)
    keep = min(WINDOWS[-1], tp)

    def last_rows(a):
        return jnp.swapaxes(a[:, :, tp - keep:], 1, 2).reshape(bp, keep, N_KV_HEADS, HEAD_DIM)

    p_cache_k, p_cache_v = last_rows(k_p), last_rows(v_p)

    def to_tm(a):
        a = jnp.swapaxes(a, -3, -2)
        return a.reshape(a.shape[:-3] + (1, a.shape[-3] * bs, a.shape[-1]))

    def from_tm(a):
        a = a.reshape(a.shape[:-3] + (a.shape[-2] // bs, bs, a.shape[-1]))
        return jnp.swapaxes(a, -3, -2)

    def to_tokens(a):
        a = a.reshape(a.shape[1], ts, bs, LANES)
        return jnp.transpose(a, (2, 1, 0, 3)).reshape(bs, ts, -1)

    def from_tokens(a):
        a = a.reshape(bs, ts, -1, LANES)
        return jnp.transpose(a, (2, 1, 0, 3)).reshape(1, -1, ts * bs, LANES)

    def attend_sample(q, k, v):
        return from_tokens(_attn_sample(to_tokens(q), to_tokens(k), to_tokens(v), cache_k, cache_v))

    pos_s = PAST_LEN + np.repeat(np.arange(ts), bs)
    y_s, k_s, v_s, s_h, s_conv, s_ffn = _trunk(
        to_tm(x_sample), pos_s, to_tm(state_rglru_conv), state_rglru_h[:, None], to_tm(state_ffn_conv),
        w, attend_sample, tstride=bs)

    return (y_prompt, from_tm(y_s), p_h[:, :, 0, :], p_conv, p_ffn, p_cache_k, p_cache_v,
            s_h[:, 0], from_tm(s_conv), from_tm(s_ffn),
            to_tokens(k_s).reshape(bs, ts, N_KV_HEADS, HEAD_DIM),
            to_tokens(v_s).reshape(bs, ts, N_KV_HEADS, HEAD_DIM))
```

```python
import functools
import math

import numpy as np
import jax
import jax.numpy as jnp
from jax import lax
from jax.experimental import pallas as pl
from jax.experimental.pallas import tpu as pltpu

F32 = jnp.float32
BF16 = jnp.bfloat16

N_LRU_BLOCKS = 4
LRU_C = 8.0
CONV_A_WIDTH = 4
FFN_CONV_WIDTH = 3
HEAD_DIM = 64
N_KV_HEADS = 8
KV_DIM = N_KV_HEADS * HEAD_DIM
WINDOWS = (128, 512, 2048)
DILATIONS = (1, 4, 16)
N_GROUPS = len(WINDOWS)
SPAN = 128
ROT_DIM = HEAD_DIM // 4
ROPE_THETA = 500000.0
PAST_LEN = 2048
EPS = 1e-6
SCALE = HEAD_DIM ** -0.5

LANES = 128
SUBLANES = 8
N_PAIRS = KV_DIM // LANES
ATT_TILE = 2048
MERGE_ROWS = 128
MIXER_AHEAD_COLS = 256
MIXER_SUB = 256
SAMPLE_ENTRIES = 2
LOG2E = 1.4426950408889634
LN2 = 0.6931471805599453
KEY_ROWS = 2 * SPAN
ROW_TILE = 512
FF_CHUNK = 3072
PROJ_CHUNK = 512
VMEM_LIMIT = 56 * 1024 * 1024

_NT = (((1,), (1,)), ((), ()))


def _rmsnorm(x, g):
    ms = jnp.mean(x * x, axis=-1, keepdims=True)
    return (x * lax.rsqrt(ms + EPS)) * g


def _dot(a, b):
    return jnp.dot(a, b, preferred_element_type=F32)


def _sigmoid(x):
    return 1.0 / (1.0 + jnp.exp2(x * (-LOG2E)))


def _gelu(x):
    k1 = -2.0 * math.sqrt(2.0 / math.pi) * LOG2E
    return x / (1.0 + jnp.exp2(x * (k1 + (k1 * 0.044715) * (x * x))))


def _const_spec(shape, single_buffer=False):
    nd = len(shape)
    kw = {"pipeline_mode": pl.Buffered(1)} if single_buffer else {}
    return pl.BlockSpec(shape, lambda *_: (0,) * nd, **kw)


def _layer_spec(stacked, layer, single_buffer=False):
    rest = stacked.shape[1:]
    kw = {"pipeline_mode": pl.Buffered(1)} if single_buffer else {}
    return pl.BlockSpec((None,) + rest, lambda *_: (layer,) + (0,) * len(rest), **kw)


def _params(*sem):
    return pltpu.CompilerParams(dimension_semantics=sem, vmem_limit_bytes=VMEM_LIMIT)


def _conv_pad(width, tstride):
    hist = (width - 1) * tstride
    return hist, max(SUBLANES, hist)


def _causal_conv(cat_ref, cols, w_ref, b_ref, width, tstride, rows):
    hist, pad = _conv_pad(width, tstride)
    acc = None
    for j in range(width):
        start = pad - hist + j * tstride
        term = cat_ref[start:start + rows, cols] * w_ref[j:j + 1, cols]
        acc = term if acc is None else acc + term
    return b_ref[:, cols] + acc


def _mixer_a_kernel(*refs, tstride, rows, sub):
    segmented = tstride == 1
    xn_ref = perm_ref = None
    if segmented:
        x_ref, xn_ref, cprev_ref, h0_ref, perm_ref = refs[:5]
        refs = refs[5:]
    else:
        x_ref, cprev_ref, h0_ref = refs[:3]
        refs = refs[3:]
    g_ref, win_ref, cw_ref, cb_ref, gw_ref, gb_ref, lam_ref, wout_ref = refs[:8]
    xo_ref, cst_ref, hl_ref, hc_s, ch_s = refs[8:13]
    nsub = rows // sub
    work = refs[13:13 + 3 * nsub]
    ybuf = refs[13 + 3 * nsub:]
    seg = SUBLANES if segmented else tstride

    @pl.when(pl.program_id(1) == 0)
    def _():
        ch_s[...] = cprev_ref[...]
        hc_s[...] = h0_ref[...]

    def normed(x):
        hn = _rmsnorm(x, g_ref[...]).astype(BF16)
        if segmented:
            hn = jnp.concatenate([_dot(perm_ref[0], hn[t * sub:(t + 1) * sub]).astype(BF16)
                                  for t in range(nsub)], axis=0)
        return hn

    def project_into(x, dst):
        hn = normed(x)
        for c in range(dst.shape[1] // MIXER_AHEAD_COLS):
            cols = slice(c * MIXER_AHEAD_COLS, (c + 1) * MIXER_AHEAD_COLS)
            dst[:, cols] = _dot(hn, win_ref[:, cols])
            yield

    def finish(y, ahead=iter(())):
        yv = []
        for t in range(nsub):
            tile = _mixer_a_tile(y, t * sub, sub, perm_ref, cw_ref, cb_ref, gw_ref, gb_ref, lam_ref, cst_ref,
                                 hl_ref, *work[3 * t:3 * t + 3], hc_s, ch_s, segmented=segmented, seg=seg)
            while True:
                try:
                    next(tile)
                except StopIteration as done:
                    yv.append(done.value)
                    break
                next(ahead, None)
        for _ in ahead:
            pass
        xo_ref[...] = x_ref[...] + _dot(jnp.concatenate(yv, axis=0), wout_ref[...])

    if not segmented:
        finish(_dot(normed(x_ref[...]), win_ref[...]))
        return

    step = pl.program_id(0) * pl.num_programs(1) + pl.program_id(1)

    @pl.when(step == 0)
    def _():
        for _ in project_into(x_ref[...], ybuf[0]):
            pass

    for slot in range(2):
        @pl.when((step & 1) == slot)
        def _(slot=slot):
            finish(ybuf[slot], project_into(xn_ref[...], ybuf[1 - slot]))


def _mixer_a_tile(y, row0, rows, perm_ref, cw_ref, cb_ref, gw_ref, gb_ref, lam_ref, cst_ref, hl_ref,
                  cat_s, a_s, u_s, hc_s, ch_s, *, segmented, seg):
    d = y.shape[1] // 2
    blk = d // N_LRU_BLOCKS
    steps = rows // seg
    taps = CONV_A_WIDTH - 1
    hist, pad = _conv_pad(CONV_A_WIDTH, seg)

    gate = y[row0:row0 + rows, :d]
    cat_s[pad:pad + rows, :] = y[row0:row0 + rows, d:]
    if segmented:
        tail = cat_s[pad + rows - hist:pad + rows, :].reshape(taps, seg, d)
        first = lax.broadcasted_iota(jnp.int32, tail.shape, 1) == 0
        carried = ch_s[...]
        cat_s[pad - hist:pad, :] = jnp.where(first, carried[:, None, :], pltpu.roll(tail, 1, 1)).reshape(hist, d)
        for k in range(taps):
            ch_s[k:k + 1, :] = cat_s[pad + rows - hist + k * seg + seg - 1:pad + rows - hist + (k + 1) * seg, :]
    else:
        cat_s[pad - hist:pad, :] = ch_s[...]
        ch_s[...] = cat_s[pad + rows - hist:pad + rows, :]
    cst_ref[...] = ch_s[...]
    xc = _causal_conv(cat_s, slice(None), cw_ref, cb_ref, CONV_A_WIDTH, seg, rows)

    lam = lam_ref[...]
    nsp = -lam
    softplus = jnp.maximum(nsp, 0.0) + jnp.log1p(jnp.exp(-jnp.abs(nsp)))
    decay = (-LRU_C * LOG2E) * softplus
    xcb = xc.astype(BF16)
    for n in range(N_LRU_BLOCKS):
        cols = slice(n * blk, (n + 1) * blk)
        ri = _dot(xcb[:, cols], gw_ref[n]) + gb_ref[n]
        r = _sigmoid(ri[:, :blk])
        i = _sigmoid(ri[:, blk:])
        a = jnp.exp2(r * decay[:, cols])
        a_s[:, cols] = a
        gain2 = 1.0 - a * a
        gain = jnp.where(gain2 > 0.0, gain2 * lax.rsqrt(gain2), 0.0)
        u_s[:, cols] = gain * (i * xc[:, cols])
        yield

    if segmented:
        h_end, c_end = jnp.zeros((seg, d), F32), jnp.ones((seg, d), F32)
        for j in range(steps):
            sl = slice(j * seg, (j + 1) * seg)
            a = a_s[sl, :]
            h_end = a * h_end + u_s[sl, :]
            c_end = a * c_end
            u_s[sl, :] = h_end
            a_s[sl, :] = c_end
        state, starts = hc_s[...], []
        for s in range(seg):
            starts.append(state)
            state = c_end[s:s + 1, :] * state + h_end[s:s + 1, :]
        start = jnp.concatenate(starts, axis=0)
        hs = (u_s[...].reshape(steps, seg, d) + a_s[...].reshape(steps, seg, d) * start[None]).reshape(rows, d)
        h_last = state
    else:
        h_last = hc_s[...]
        for t in range(steps):
            sl = slice(t * seg, (t + 1) * seg)
            h_last = a_s[sl, :] * h_last + u_s[sl, :]
            u_s[sl, :] = h_last
        hs = u_s[...]
    hc_s[...] = h_last
    hl_ref[...] = h_last
    yield

    yv = (hs * _gelu(gate)).astype(BF16)
    if segmented:
        yv = _dot(perm_ref[1], yv).astype(BF16)
    return yv


def _mixer_a(x, cprev, h0, g, win, cw, cb, gw, gb, lam, wout, *, layer, tstride):
    nb, t, d = x.shape
    ls = functools.partial(_layer_spec, layer=layer)
    rows = min(ROW_TILE, t)
    hist = (CONV_A_WIDTH - 1) * tstride
    seg = SUBLANES if tstride == 1 else tstride
    _, pad = _conv_pad(CONV_A_WIDTH, seg)
    row_spec = pl.BlockSpec((None, rows, d), lambda b, i: (b, i, 0))
    per_b = lambda n: pl.BlockSpec((None, n, d), lambda b, i: (b, 0, 0))
    tiles = t // rows
    x_args, x_specs, perm_args, perm_specs, ybuf = (x,), [row_spec], (), [], []
    sub, semantics = rows, ("parallel", "arbitrary")
    if tstride == 1:
        x_args = (x, x.reshape(nb * t, d))
        x_specs = [row_spec, pl.BlockSpec((rows, d), lambda b, i: (jnp.minimum(b * tiles + i + 1, nb * tiles - 1), 0))]
        sub = min(MIXER_SUB, rows)
        time_of_row = (np.arange(sub) % seg) * (sub // seg) + np.arange(sub) // seg
        perm = np.zeros((sub, sub), np.float32)
        perm[np.arange(sub), time_of_row] = 1.0
        perm_args = (jnp.asarray(np.stack([perm, perm.T]), BF16),)
        perm_specs = [_const_spec((2, sub, sub))]
        ybuf = [pltpu.VMEM((rows, 2 * d), F32)] * 2
        semantics = ("arbitrary", "arbitrary")
    return pl.pallas_call(
        functools.partial(_mixer_a_kernel, tstride=tstride, rows=rows, sub=sub),
        grid=(nb, tiles),
        in_specs=x_specs + [per_b(hist), per_b(tstride)] + perm_specs
        + [ls(g), ls(win, single_buffer=True), ls(cw), ls(cb), ls(gw, single_buffer=True), ls(gb),
           ls(lam), ls(wout, single_buffer=True)],
        out_specs=[row_spec, per_b(hist), per_b(tstride)],
        out_shape=[jax.ShapeDtypeStruct(x.shape, F32),
                   jax.ShapeDtypeStruct((nb, hist, d), F32),
                   jax.ShapeDtypeStruct((nb, tstride, d), F32)],
        scratch_shapes=[pltpu.VMEM((tstride, d), F32), pltpu.VMEM((hist, d), F32)]
        + [pltpu.VMEM((pad + sub, d), F32), pltpu.VMEM((sub, d), F32), pltpu.VMEM((sub, d), F32)] * (rows // sub)
        + ybuf,
        compiler_params=_params(*semantics),
        name="mixer_a",
    )(*x_args, cprev, h0, *perm_args, g, win, cw, cb, gw, gb, lam, wout)


def _ffn_kernel(*refs, tstride, rows, has_att):
    x_ref = refs[0]
    if has_att:
        att_ref, wo_ref = refs[1:3]
    prev_ref, g_ref, wup_ref, cw_ref, cb_ref, wdn_ref, xo_ref, st_ref, cat_s = refs[1 + 2 * has_att:]
    dff = wdn_ref.shape[0]
    hist, pad = _conv_pad(FFN_CONV_WIDTH, tstride)

    @pl.when(pl.program_id(1) == 0)
    def _():
        cat_s[pad - hist:pad, :] = prev_ref[...]

    x = x_ref[...]
    if has_att:
        merged = jnp.concatenate([att_ref[p] for p in range(att_ref.shape[0])], axis=1)
        x = x + _dot(merged.astype(BF16), wo_ref[...])

    hn = _rmsnorm(x, g_ref[...]).astype(BF16)
    acc = None
    for c in range(dff // FF_CHUNK):
        cols = slice(c * FF_CHUNK, (c + 1) * FF_CHUNK)
        vcols = slice(dff + c * FF_CHUNK, dff + (c + 1) * FF_CHUNK)
        cat_s[pad:pad + rows, cols] = _dot(hn, wup_ref[:, cols])
        gc = _causal_conv(cat_s, cols, cw_ref, cb_ref, FFN_CONV_WIDTH, tstride, rows)
        act = (_gelu(gc) * _dot(hn, wup_ref[:, vcols])).astype(BF16)
        part = _dot(act, wdn_ref[cols, :])
        acc = part if acc is None else acc + part
    new_hist = cat_s[pad + rows - hist:pad + rows, :]
    cat_s[pad - hist:pad, :] = new_hist
    st_ref[...] = new_hist
    xo_ref[...] = x + acc


def _conv_ffn(x, att, wo, prev, g, wup, cw, cb, wdn, *, layer, att_layer, tstride):
    nb, t, d = x.shape
    dff = wdn.shape[1]
    ls = functools.partial(_layer_spec, layer=layer)
    rows = min(ROW_TILE, t)
    hist, pad = _conv_pad(FFN_CONV_WIDTH, tstride)
    row_spec = pl.BlockSpec((None, rows, d), lambda b, i: (b, i, 0))
    st_spec = pl.BlockSpec((None, hist, dff), lambda b, i: (b, 0, 0))
    has_att = att is not None
    att_args, att_specs = (), []
    if has_att:
        att_args = (att, wo)
        att_specs = [pl.BlockSpec((None, N_PAIRS, rows, LANES), lambda b, i: (b, 0, i, 0)),
                     _layer_spec(wo, att_layer)]
    return pl.pallas_call(
        functools.partial(_ffn_kernel, tstride=tstride, rows=rows, has_att=has_att),
        grid=(nb, t // rows),
        in_specs=[row_spec] + att_specs
        + [st_spec, ls(g), ls(wup, single_buffer=True), ls(cw), ls(cb), ls(wdn, single_buffer=True)],
        out_specs=[row_spec, st_spec],
        out_shape=[jax.ShapeDtypeStruct(x.shape, F32), jax.ShapeDtypeStruct((nb, hist, dff), F32)],
        scratch_shapes=[pltpu.VMEM((pad + rows, dff), F32)],
        compiler_params=_params("parallel", "arbitrary"),
        name="conv_ffn",
    )(x, *att_args, prev, g, wup, cw, cb, wdn)


def _proj_kernel(x_ref, g_ref, w_ref, hsum_ref, gh_ref, cos_ref, sin_ref, *out_refs, n_rope):
    hn = _rmsnorm(x_ref[...], g_ref[...]).astype(BF16)
    rows = hn.shape[0]
    lane = lax.broadcasted_iota(jnp.int32, (rows, LANES), 1)
    first_half = (lane & (HEAD_DIM - 1)) < (ROT_DIM // 2)
    hsum = hsum_ref[...]
    cos = cos_ref[...]
    sin = sin_ref[...]
    for c in range(w_ref.shape[1] // PROJ_CHUNK):
        y = _dot(hn, w_ref[:, c * PROJ_CHUNK:(c + 1) * PROJ_CHUNK])
        for j in range(PROJ_CHUNK // LANES):
            i = c * (PROJ_CHUNK // LANES) + j
            yt = y[:, j * LANES:(j + 1) * LANES]
            if i >= n_rope // LANES:
                out_refs[1][i - n_rope // LANES] = yt
                continue
            sq = yt * yt
            hi = sq.astype(BF16)
            lo = (sq - hi.astype(F32)).astype(BF16)
            ms = _dot(jnp.concatenate([hi, lo], axis=1), hsum)
            yn = (yt * lax.rsqrt(ms + EPS)) * gh_ref[...]
            partner = jnp.where(first_half, pltpu.roll(yn, LANES - ROT_DIM // 2, 1),
                                pltpu.roll(yn, ROT_DIM // 2, 1))
            out_refs[0][i] = yn * cos + partner * sin


def _project(x, g, w, hsum, gh, cos, sin, *, layer, n_rope):
    nb, t, d = x.shape
    n = w.shape[2]
    rows = min(ROW_TILE, t)
    row_in = pl.BlockSpec((None, rows, d), lambda b, i: (b, i, 0))
    tab = pl.BlockSpec((rows, LANES), lambda b, i: (i, 0))
    tiles = [n_rope // LANES] + ([(n - n_rope) // LANES] if n > n_rope else [])
    return pl.pallas_call(
        functools.partial(_proj_kernel, n_rope=n_rope),
        grid=(nb, t // rows),
        in_specs=[row_in, _layer_spec(g, layer), _layer_spec(w, layer), _const_spec(hsum.shape),
                  _layer_spec(gh, layer), tab, tab],
        out_specs=[pl.BlockSpec((None, nt, rows, LANES), lambda b, i: (b, 0, i, 0)) for nt in tiles],
        out_shape=[jax.ShapeDtypeStruct((nb, nt, t, LANES), F32) for nt in tiles],
        compiler_params=_params("parallel", "parallel"),
        name="norm_proj_rope",
    )(x, g, w, hsum, gh, cos, sin)


def _attn_group(q_ref, k_ref, v_ref, og_s, lg_s, tile_start, dil, tile):
    per_res = tile // dil
    nq = min(SPAN, per_res)
    nsub = per_res // nq
    sub0 = tile_start // dil
    qi = lax.broadcasted_iota(jnp.int32, (2 * nq, KEY_ROWS), 0) & (nq - 1)
    kj = lax.broadcasted_iota(jnp.int32, (2 * nq, KEY_ROWS), 1)
    rel = qi - kj
    low = lax.broadcasted_iota(jnp.int32, (nq, LANES), 1) < HEAD_DIM
    ones = jnp.ones((KEY_ROWS, LANES), BF16)

    def rows(start, n):
        return pl.ds(start, n, stride=dil) if dil > 1 else pl.ds(start, n)

    def block(idx, carry):
        r = idx // nsub
        j = idx - r * nsub
        q0 = sub0 + j * nq
        k0 = jnp.maximum(q0 + nq - KEY_ROWS, 0)
        dist = rel + (q0 - k0)
        bias = jnp.where((dist >= 0) & (dist <= SPAN), 0.0, -jnp.inf)
        q_rows = rows(r + dil * (j * nq), nq)
        k_rows = rows(r + dil * k0, KEY_ROWS)
        for p in range(N_PAIRS):
            qp = (q_ref[p, q_rows, :] * (SCALE * LOG2E)).astype(BF16)
            zero = jnp.zeros_like(qp)
            qs = jnp.concatenate([jnp.where(low, qp, zero), jnp.where(low, zero, qp)], axis=0)
            s = lax.dot_general(qs, k_ref[p, k_rows, :].astype(BF16), _NT, preferred_element_type=F32)
            s = s + bias
            m = jnp.max(s, axis=-1, keepdims=True)
            pe = jnp.exp2(s - m).astype(BF16)
            od = _dot(pe, jnp.concatenate([v_ref[p, k_rows, :].astype(BF16), ones], axis=1))
            den = od[:, LANES:]
            o = od[:, :LANES] / den
            lse = m * LN2 + jnp.log(den)
            og_s[p, q_rows, :] = jnp.where(low, o[:nq], o[nq:])
            lg_s[p, q_rows, :] = jnp.where(low, lse[:nq], lse[nq:])
        return carry

    lax.fori_loop(0, dil * nsub, block, 0, unroll=4)


def _attn_prompt_kernel(q_ref, k_ref, v_ref, o_ref, og_s, lg_s, la_s, *, tile):
    group = pl.program_id(2)
    tile_start = pl.program_id(1) * tile
    for gi, dil in enumerate(DILATIONS):
        @pl.when(group == gi)
        def _(gi=gi, dil=dil):
            dst = (o_ref, la_s) if gi == 0 else (og_s, lg_s)
            _attn_group(q_ref, k_ref, v_ref, *dst, tile_start, dil, tile)

    @pl.when(group > 0)
    def _():
        def merge(c, carry):
            rows = pl.ds(pl.multiple_of(c * MERGE_ROWS, MERGE_ROWS), MERGE_ROWS)
            for p in range(N_PAIRS):
                la, lg = la_s[p, rows, :], lg_s[p, rows, :]
                top = jnp.maximum(la, lg)
                ea, eg = jnp.exp(la - top), jnp.exp(lg - top)
                tot = ea + eg
                o_ref[p, rows, :] = (ea * o_ref[p, rows, :] + eg * og_s[p, rows, :]) / tot
                la_s[p, rows, :] = top + jnp.log(tot)
            return carry

        lax.fori_loop(0, tile // MERGE_ROWS, merge, 0)


def _attn_prompt(q, k, v):
    nb, _, t, _ = k.shape
    tile = min(ATT_TILE, t)
    assert tile % (DILATIONS[-1] * SUBLANES) == 0 and t % tile == 0 and t >= DILATIONS[-1] * KEY_ROWS
    whole = pl.BlockSpec((None, N_PAIRS, t, LANES), lambda b, i, g: (b, 0, 0, 0),
                         pipeline_mode=pl.Buffered(1))
    scratch = pltpu.VMEM((N_PAIRS, tile, LANES), F32)
    return pl.pallas_call(
        functools.partial(_attn_prompt_kernel, tile=tile),
        grid=(nb, t // tile, N_GROUPS),
        in_specs=[pl.BlockSpec((None, N_PAIRS, tile, LANES), lambda b, i, g: (b, g, i, 0)), whole, whole],
        out_specs=pl.BlockSpec((None, N_PAIRS, tile, LANES), lambda b, i, g: (b, 0, i, 0)),
        out_shape=jax.ShapeDtypeStruct(k.shape, F32),
        scratch_shapes=[scratch] * 3,
        compiler_params=_params("parallel", "arbitrary", "arbitrary"),
        name="attn_prompt",
    )(q, k, v)


def _attn_sample_kernel(q_ref, kn_ref, vn_ref, kt_ref, vt_ref, o_ref, kn_s, vn_s):
    for i in range(q_ref.shape[0]):
        _attn_sample_entry(q_ref.at[i], kn_ref.at[i], vn_ref.at[i], kt_ref.at[i], vt_ref.at[i],
                           o_ref.at[i], kn_s, vn_s)


def _attn_sample_entry(q_ref, kn_ref, vn_ref, kt_ref, vt_ref, o_ref, kn_s, vn_s):
    steps, past = q_ref.shape[0], kt_ref.shape[-1]
    nrow = steps * N_KV_HEADS
    kn_s[...] = jnp.zeros_like(kn_s)
    vn_s[...] = jnp.zeros_like(vn_s)
    kn_s[0:steps, :] = kn_ref[...]
    vn_s[0:steps, :] = vn_ref[...]
    kn = kn_s[...].astype(BF16)
    vn = vn_s[...].astype(BF16)
    kt = kt_ref[...].reshape(KV_DIM, past).astype(BF16)
    vt = vt_ref[...].reshape(KV_DIM, past).astype(BF16)

    row = lax.broadcasted_iota(jnp.int32, (nrow, KV_DIM), 0)
    lane = lax.broadcasted_iota(jnp.int32, (nrow, KV_DIM), 1)
    head_lanes = (lane >> 6) == (row & (N_KV_HEADS - 1))
    q = q_ref[...] * SCALE

    def q_rows(g):
        qg = q[:, g * KV_DIM:(g + 1) * KV_DIM]
        rep = jnp.concatenate(
            [jnp.broadcast_to(qg[s:s + 1, :], (N_KV_HEADS, KV_DIM)) for s in range(steps)], axis=0)
        return jnp.where(head_lanes, rep, 0.0)

    qbd = jnp.concatenate([q_rows(g) for g in range(N_GROUPS)], axis=0).astype(BF16)
    s_win = _dot(qbd, kt)
    s_new = lax.dot_general(qbd, kn, _NT, preferred_element_type=F32)

    def masked(scores, first_pos):
        width = scores.shape[1]
        srow = lax.broadcasted_iota(jnp.int32, (nrow, width), 0) >> 3
        col = lax.broadcasted_iota(jnp.int32, (nrow, width), 1)
        dist = (past - first_pos) + srow - col
        in_cache = col < (past + steps - first_pos)
        out = []
        for g, dil in enumerate(DILATIONS):
            ok = (dist >= 0) & (dist <= SPAN * dil) & ((dist & (dil - 1)) == 0) & in_cache
            out.append(jnp.where(ok, scores[g * nrow:(g + 1) * nrow], -jnp.inf))
        return out

    parts = masked(s_win, 0) + masked(s_new, past)
    top = functools.reduce(jnp.maximum, [jnp.max(s, axis=-1, keepdims=True) for s in parts])
    probs = [jnp.exp(s - top) for s in parts]
    den = functools.reduce(jnp.add, [jnp.sum(p, axis=-1, keepdims=True) for p in probs])
    p_win = jnp.concatenate(probs[:N_GROUPS], axis=0).astype(BF16)
    p_new = jnp.concatenate(probs[N_GROUPS:], axis=0).astype(BF16)
    acc = lax.dot_general(p_win, vt, _NT, preferred_element_type=F32) + _dot(p_new, vn)
    acc = functools.reduce(jnp.add, [acc[g * nrow:(g + 1) * nrow] for g in range(N_GROUPS)])
    acc = jnp.where(head_lanes, acc / den, 0.0)
    o_ref[...] = jnp.sum(acc.reshape(steps, N_KV_HEADS, KV_DIM), axis=1)


def _attn_sample(q, kn, vn, cache_k, cache_v):
    nb, steps, _ = kn.shape
    past, heads, hd = cache_k.shape[1:]
    assert past % LANES == 0 and steps <= SUBLANES, (past, steps)
    views = [jnp.transpose(c, (0, 2, 3, 1)) for c in (cache_k, cache_v)]
    per = SAMPLE_ENTRIES if nb % SAMPLE_ENTRIES == 0 else 1
    cache_spec = pl.BlockSpec((per, heads, hd, past), lambda b: (b, 0, 0, 0))
    new_spec = pl.BlockSpec((per, steps, KV_DIM), lambda b: (b, 0, 0))
    return pl.pallas_call(
        _attn_sample_kernel,
        grid=(nb // per,),
        in_specs=[pl.BlockSpec((per, steps, N_GROUPS * KV_DIM), lambda b: (b, 0, 0)),
                  new_spec, new_spec, cache_spec, cache_spec],
        out_specs=new_spec,
        out_shape=jax.ShapeDtypeStruct(kn.shape, F32),
        scratch_shapes=[pltpu.VMEM((SPAN, KV_DIM), F32), pltpu.VMEM((SPAN, KV_DIM), F32)],
        compiler_params=_params("parallel"),
        name="attn_sample",
    )(q, kn, vn, *views)


def _rope_tables(pos):
    half = ROT_DIM // 2
    inv = ROPE_THETA ** (-np.arange(half, dtype=np.float64) * (2.0 / ROT_DIM))
    dlane = np.arange(LANES) % HEAD_DIM
    inv_lane = np.where(dlane < ROT_DIM, inv[dlane % half], 0.0)
    ang = np.asarray(pos, np.float64)[:, None] * inv_lane[None, :]
    sign = np.where(dlane < half, -1.0, 1.0)
    return jnp.asarray(np.cos(ang), F32), jnp.asarray(np.sin(ang) * sign, F32)


def _trunk(x, pos, conv_a_prev, h0, ffn_prev, w, attend, *, tstride):
    n_a = w["w_in"].shape[0]
    depth = w["w_up"].shape[0]
    cos, sin = _rope_tables(pos)
    hs_new, conv_new, ffn_new = [], [], []
    k = v = None
    for layer in range(depth):
        att, j = None, max(layer - n_a, 0)
        if layer < n_a:
            x, cs, hl = _mixer_a(x, conv_a_prev[layer], h0[layer], w["norm_a"], w["w_in"], w["conv_a_w"],
                                 w["conv_a_b"], w["gate_w"], w["gate_b"], w["lam"], w["w_out"],
                                 layer=layer, tstride=tstride)
            conv_new.append(cs)
            hs_new.append(hl)
        else:
            (q,) = _project(x, w["norm_b"], w["w_q"], w["hsum"], w["q_norm"], cos, sin,
                            layer=j, n_rope=w["w_q"].shape[-1])
            att = attend(q, k, v)
        x, fs = _conv_ffn(x, att, w["w_o"], ffn_prev[layer], w["norm_ffn"], w["w_up"], w["ffn_conv_w"],
                          w["ffn_conv_b"], w["w_down"], layer=layer, att_layer=j, tstride=tstride)
        ffn_new.append(fs)
        if layer == n_a - 1:
            k, v = _project(x, w["norm_kv"], w["w_kv"], w["hsum"], w["k_norm"], cos, sin,
                            layer=0, n_rope=KV_DIM)
    return x, k, v, jnp.stack(hs_new), jnp.stack(conv_new), jnp.stack(ffn_new)


def kernel(x_prompt, x_sample, cache_k, cache_v, state_rglru_h, state_rglru_conv, state_ffn_conv, norm_mix_a, w_in_a, conv_a_w, conv_a_b, gate_r_w, gate_r_b, gate_i_w, gate_i_b, lru_lambda, w_out_a, norm_kv, w_kv, k_norm, norm_mix_b, w_q, q_norm, w_o, norm_ffn, w_ffn_up, ffn_conv_w, ffn_conv_b, w_ffn_down):
    n_a, d = norm_mix_a.shape
    depth, _, dff2 = w_ffn_up.shape
    dff = dff2 // 2
    bp, tp, _ = x_prompt.shape
    bs, ts, _ = x_sample.shape

    head_of = np.arange(LANES) // HEAD_DIM
    w = dict(
        norm_a=norm_mix_a[:, None, :], w_in=w_in_a.astype(BF16),
        conv_a_w=conv_a_w, conv_a_b=conv_a_b[:, None, :],
        gate_w=jnp.concatenate([gate_r_w, gate_i_w], axis=-1).astype(BF16),
        gate_b=jnp.concatenate([gate_r_b, gate_i_b], axis=-1)[:, :, None, :],
        lam=lru_lambda[:, None, :], w_out=w_out_a.astype(BF16),
        norm_kv=norm_kv[None, None, :], w_kv=w_kv.astype(BF16)[None],
        k_norm=jnp.tile(k_norm, LANES // HEAD_DIM)[None, None, :],
        norm_b=norm_mix_b[:, None, :], w_q=w_q.astype(BF16),
        q_norm=jnp.tile(q_norm, (1, LANES // HEAD_DIM))[:, None, :],
        w_o=w_o.astype(BF16), norm_ffn=norm_ffn[:, None, :],
        w_up=w_ffn_up.astype(BF16), ffn_conv_w=ffn_conv_w, ffn_conv_b=ffn_conv_b[:, None, :],
        w_down=w_ffn_down.astype(BF16),
        hsum=jnp.asarray(np.tile((head_of[:, None] == head_of[None, :]) / HEAD_DIM, (2, 1)), BF16),
    )

    y_prompt, k_p, v_p, p_h, p_conv, p_ffn = _trunk(
        x_prompt, np.arange(tp),
        jnp.zeros((n_a, bp, CONV_A_WIDTH - 1, d), F32), jnp.zeros((n_a, bp, 1, d), F32),
        jnp.zeros((depth, bp, FFN_CONV_WIDTH - 1, dff), F32), w, _attn_prompt, tstride=1)
    keep = min(WINDOWS[-1], tp)

    def last_rows(a):
        return jnp.swapaxes(a[:, :, tp - keep:], 1, 2).reshape(bp, keep, N_KV_HEADS, HEAD_DIM)

    p_cache_k, p_cache_v = last_rows(k_p), last_rows(v_p)

    def to_tm(a):
        a = jnp.swapaxes(a, -3, -2)
        return a.reshape(a.shape[:-3] + (1, a.shape[-3] * bs, a.shape[-1]))

    def from_tm(a):
        a = a.reshape(a.shape[:-3] + (a.shape[-2] // bs, bs, a.shape[-1]))
        return jnp.swapaxes(a, -3, -2)

    def to_tokens(a):
        a = a.reshape(a.shape[1], ts, bs, LANES)
        return jnp.transpose(a, (2, 1, 0, 3)).reshape(bs, ts, -1)

    def from_tokens(a):
        a = a.reshape(bs, ts, -1, LANES)
        return jnp.transpose(a, (2, 1, 0, 3)).reshape(1, -1, ts * bs, LANES)

    def attend_sample(q, k, v):
        return from_tokens(_attn_sample(to_tokens(q), to_tokens(k), to_tokens(v), cache_k, cache_v))

    pos_s = PAST_LEN + np.repeat(np.arange(ts), bs)
    y_s, k_s, v_s, s_h, s_conv, s_ffn = _trunk(
        to_tm(x_sample), pos_s, to_tm(state_rglru_conv), state_rglru_h[:, None], to_tm(state_ffn_conv),
        w, attend_sample, tstride=bs)

    return (y_prompt, from_tm(y_s), p_h[:, :, 0, :], p_conv, p_ffn, p_cache_k, p_cache_v,
            s_h[:, 0], from_tm(s_conv), from_tm(s_ffn),
            to_tokens(k_s).reshape(bs, ts, N_KV_HEADS, HEAD_DIM),
            to_tokens(v_s).reshape(bs, ts, N_KV_HEADS, HEAD_DIM))
```
